```python
import math
import jax
import jax.numpy as jnp
from jax import lax
import numpy as np

D_MODEL = 1024
BATCH = 8
SEQ = 4096
DEPTH = 4

N_BRANCH = 5
BRANCH_W = D_MODEL // 4
HG_HEADS = 4
HG_DK = BRANCH_W // HG_HEADS
HG_CHUNK = 64
CONF_K = 31
SC_K = 3
AT_HEADS = 4
AT_KV_HEADS = 2
AT_HD = BRANCH_W // AT_HEADS
AT_BLOCK = 128
AT_WINDOW = 128
REL_BUCKETS = 32
REL_MAX_DIST = 128
MEM_LEN = 256
MEM_HEADS = 4
MEM_HD = BRANCH_W // MEM_HEADS
EPS = 1e-6

IN_COLS = (
    BRANCH_W, BRANCH_W, BRANCH_W, BRANCH_W, BRANCH_W,
    BRANCH_W, BRANCH_W, BRANCH_W,
    BRANCH_W, BRANCH_W, BRANCH_W, BRANCH_W,
    BRANCH_W, AT_KV_HEADS * AT_HD, AT_KV_HEADS * AT_HD, BRANCH_W,
    BRANCH_W, BRANCH_W,
)
IN_WIDTH = sum(IN_COLS)

kernel_name = 'hybrid_parallel_gated_encoder'

F32 = jnp.float32


def rmsnorm(x, g):
    xf = x.astype(F32)
    y = xf * lax.rsqrt(jnp.mean(xf * xf, axis=-1, keepdims=True) + EPS) * g.astype(F32)
    return y.astype(x.dtype)


def split_cols(u):
    offs = np.cumsum(IN_COLS)[:-1].tolist()
    return jnp.split(u, offs, axis=-1)


def depthwise_conv(x, w):
    k = w.shape[0]
    return lax.conv_general_dilated(
        x, w[:, None, :].astype(x.dtype), window_strides=(1,),
        padding=[((k - 1) // 2, (k - 1) // 2)],
        dimension_numbers=('NWC', 'WIO', 'NWC'),
        feature_group_count=x.shape[-1])


def gla_scan(q, k, v, log_f):
    B_, S_, H, dk = q.shape
    dv = v.shape[-1]
    nc = S_ // HG_CHUNK

    def chunks(t):
        return t.reshape(B_, nc, HG_CHUNK, H, t.shape[-1]).transpose(1, 0, 3, 2, 4)

    incl = jnp.tril(jnp.ones((HG_CHUNK, HG_CHUNK), bool))[..., None]

    def step(state, blk):
        qc, kc, vc, gc = blk
        a = jnp.cumsum(gc, axis=2)
        diff = a[:, :, :, None, :] - a[:, :, None, :, :]
        decay = jnp.exp(jnp.where(incl, diff, -jnp.inf))
        scores = jnp.einsum('bhtk,bhsk,bhtsk->bhts', qc, kc, decay)
        o = (jnp.einsum('bhts,bhsv->bhtv', scores, vc)
             + jnp.einsum('bhtk,bhkv->bhtv', qc * jnp.exp(a), state))
        a_end = a[:, :, -1:, :]
        state = (state * jnp.exp(a_end[:, :, 0, :, None])
                 + jnp.einsum('bhsk,bhsv->bhkv', kc * jnp.exp(a_end - a), vc))
        return state, o

    init = jnp.zeros((B_, H, dk, dv), F32)
    _, oc = lax.scan(step, init, (chunks(q), chunks(k), chunks(v), chunks(log_f)))
    return oc.transpose(1, 0, 3, 2, 4).reshape(B_, S_, H, dv)


def hgrn2_branch(q, f_fwd, f_bwd, i, z, lb, onorm):
    B_, S_, _ = q.shape

    def heads(t):
        return t.astype(F32).reshape(B_, S_, HG_HEADS, HG_DK)

    qh, ih = heads(q), heads(i)

    def direction(f_logit, lb_d):
        f = lb_d + (1.0 - lb_d) * jax.nn.sigmoid(f_logit.astype(F32))
        return heads(1.0 - f), heads(jnp.log(f))

    k_f, g_f = direction(f_fwd, lb[0])
    k_b, g_b = direction(f_bwd, lb[1])
    o_fwd = gla_scan(qh, k_f, ih, g_f)
    rev = lambda t: jnp.flip(t, axis=1)
    o_bwd = rev(gla_scan(rev(qh), rev(k_b), rev(ih), rev(g_b)))
    o = rmsnorm(o_fwd + o_bwd, onorm).reshape(B_, S_, BRANCH_W).astype(q.dtype)
    return o * jax.nn.silu(z)


def conformer_branch(a, b, z, w_dw, b_dw, ln_g, ln_b):
    u = a * jax.nn.sigmoid(b)
    c = (depthwise_conv(u, w_dw) + b_dw.astype(u.dtype)).astype(F32)
    mu = jnp.mean(c, axis=-1, keepdims=True)
    var = jnp.mean(jnp.square(c - mu), axis=-1, keepdims=True)
    n = (c - mu) * lax.rsqrt(var + EPS) * ln_g.astype(F32) + ln_b.astype(F32)
    return jax.nn.silu(n).astype(a.dtype) * jax.nn.silu(z)


def short_conv_branch(b, c, v, z, w):
    return b * depthwise_conv(c * v, w) * jax.nn.silu(z)


def t5_bucket(rel):
    half = REL_BUCKETS // 2
    n = -rel
    ret = jnp.where(n < 0, half, 0)
    n = jnp.abs(n)
    max_exact = half // 2
    large = max_exact + (jnp.log(jnp.maximum(n, 1).astype(F32) / max_exact)
                         / math.log(REL_MAX_DIST / max_exact)
                         * (half - max_exact)).astype(jnp.int32)
    large = jnp.minimum(large, half - 1)
    return ret + jnp.where(n < max_exact, n, large)


def banded_position_bias(rel_bias):
    qi = jnp.arange(AT_BLOCK)[:, None]
    sj = jnp.arange(3 * AT_BLOCK)[None, :]
    rel = (sj - AT_BLOCK) - qi
    return jnp.transpose(rel_bias[t5_bucket(rel)], (2, 0, 1)).astype(F32)


def window_attention(q, k, v, sink, pos_bias):
    B_, S_, _ = q.shape
    nb = S_ // AT_BLOCK
    G = AT_HEADS // AT_KV_HEADS
    qb = q.reshape(B_, nb, AT_BLOCK, AT_KV_HEADS, G, AT_HD)

    def band(t):
        tp = jnp.pad(t.reshape(B_, S_, AT_KV_HEADS, AT_HD),
                     ((0, 0), (AT_BLOCK, AT_BLOCK), (0, 0), (0, 0)))
        tp = tp.reshape(B_, nb + 2, AT_BLOCK, AT_KV_HEADS, AT_HD)
        return jnp.concatenate([tp[:, :-2], tp[:, 1:-1], tp[:, 2:]], axis=2)

    kb, vb = band(k), band(v)
    logits = jnp.einsum('bnqkgd,bnskd->bnkgqs', qb, kb).astype(F32) * AT_HD ** -0.5
    logits = logits + pos_bias.reshape(AT_KV_HEADS, G, AT_BLOCK, 3 * AT_BLOCK)
    qi = jnp.arange(AT_BLOCK)[None, :, None]
    sj = jnp.arange(3 * AT_BLOCK)[None, None, :]
    blk = jnp.arange(nb)[:, None, None]
    qpos = blk * AT_BLOCK + qi
    kpos = (blk - 1) * AT_BLOCK + sj
    valid = (jnp.abs(kpos - qpos) <= AT_WINDOW) & (kpos >= 0) & (kpos < S_)
    logits = jnp.where(valid[None, :, None, None], logits, -jnp.inf)
    sink_col = jnp.broadcast_to(sink.astype(F32).reshape(1, 1, AT_KV_HEADS, G, 1, 1),
                                logits.shape[:-1] + (1,))
    p = jax.nn.softmax(jnp.concatenate([logits, sink_col], axis=-1), axis=-1)[..., :-1]
    o = jnp.einsum('bnkgqs,bnskd->bnqkgd', p.astype(v.dtype), vb)
    return o.reshape(B_, S_, BRANCH_W)


def memory_cross_attention(q, mem_kv):
    B_, S_, _ = q.shape
    M = mem_kv.shape[1]
    k, v = jnp.split(mem_kv, 2, axis=-1)
    qh = q.reshape(B_, S_, MEM_HEADS, MEM_HD)
    kh = k.reshape(B_, M, MEM_HEADS, MEM_HD)
    vh = v.reshape(B_, M, MEM_HEADS, MEM_HD)
    logits = jnp.einsum('bshd,bmhd->bhsm', qh, kh).astype(F32) * MEM_HD ** -0.5
    p = jax.nn.softmax(logits, axis=-1).astype(v.dtype)
    return jnp.einsum('bhsm,bmhd->bshd', p, vh).reshape(B_, S_, BRANCH_W)


def setup_inputs(seed: int = 0) -> dict:
    key = jax.random.key(seed)
    ks = jax.random.split(key, 20)

    def nrm(k, shape, scale):
        return jax.random.normal(k, shape, F32) * scale

    W = BRANCH_W
    return {
        'x': nrm(ks[0], (BATCH, SEQ, D_MODEL), 1.0),
        'mem': nrm(ks[1], (BATCH, MEM_LEN, D_MODEL), 1.0),
        'norm_pre': 1.0 + nrm(ks[2], (DEPTH, D_MODEL), 0.05),
        'norm_post': 1.0 + nrm(ks[3], (DEPTH, D_MODEL), 0.05),
        'w_in': nrm(ks[4], (DEPTH, D_MODEL, IN_WIDTH), D_MODEL ** -0.5),
        'hg_lb_logits': nrm(ks[5], (DEPTH, 2, W), 0.1),
        'hg_onorm': 1.0 + nrm(ks[6], (DEPTH, HG_DK), 0.05),
        'conf_dw_w': nrm(ks[7], (DEPTH, CONF_K, W), CONF_K ** -0.5),
        'conf_dw_b': nrm(ks[8], (DEPTH, W), 0.02),
        'conf_ln_g': 1.0 + nrm(ks[9], (DEPTH, W), 0.05),
        'conf_ln_b': nrm(ks[10], (DEPTH, W), 0.02),
        'sc_w': nrm(ks[11], (DEPTH, SC_K, W), SC_K ** -0.5),
        'attn_sink': nrm(ks[12], (DEPTH, AT_HEADS), 0.5),
        'rel_bias': nrm(ks[13], (REL_BUCKETS, AT_HEADS), 0.5),
        'mem_norm': 1.0 + nrm(ks[14], (DEPTH, D_MODEL), 0.05),
        'w_mem_kv': nrm(ks[15], (DEPTH, D_MODEL, 2 * BRANCH_W), D_MODEL ** -0.5),
        'w_gate': nrm(ks[16], (DEPTH, N_BRANCH, D_MODEL, D_MODEL), D_MODEL ** -0.5),
        'w_branch': nrm(ks[17], (DEPTH, N_BRANCH, W, D_MODEL), W ** -0.5),
        'w_out': nrm(ks[18], (DEPTH, D_MODEL, D_MODEL), D_MODEL ** -0.5),
    }


def reference(x, mem, norm_pre, norm_post, w_in, hg_lb_logits, hg_onorm, conf_dw_w, conf_dw_b,
              conf_ln_g, conf_ln_b, sc_w, attn_sink, rel_bias, mem_norm, w_mem_kv, w_gate,
              w_branch, w_out):
    lb_soft = jax.nn.softmax(hg_lb_logits.astype(F32), axis=0)
    lower_bounds = jnp.cumsum(lb_soft, axis=0) - lb_soft[0]
    pos_bias = banded_position_bias(rel_bias)
    for l in range(DEPTH):
        h = rmsnorm(x, norm_pre[l])
        (hq, hf_f, hf_b, hi, hz, ca, cb, cz, sb, sc, sv, sz,
         aq, ak, av, az, mq, mz) = split_cols(h @ w_in[l])
        y_hg = hgrn2_branch(hq, hf_f, hf_b, hi, hz, lower_bounds[l], hg_onorm[l])
        y_cf = conformer_branch(ca, cb, cz, conf_dw_w[l], conf_dw_b[l], conf_ln_g[l], conf_ln_b[l])
        y_sc = short_conv_branch(sb, sc, sv, sz, sc_w[l])
        y_at = window_attention(aq, ak, av, attn_sink[l], pos_bias) * jax.nn.silu(az)
        mem_kv = rmsnorm(mem, mem_norm[l]) @ w_mem_kv[l]
        y_mem = memory_cross_attention(mq, mem_kv) * jax.nn.silu(mz)
        merged = None
        for n, yb in enumerate((y_hg, y_cf, y_sc, y_at, y_mem)):
            term = jax.nn.sigmoid(h @ w_gate[l, n]) * (yb @ w_branch[l, n])
            merged = term if merged is None else merged + term
        x = x + rmsnorm(merged @ w_out[l], norm_post[l])
    return x
```

```python
import functools
import math

import jax
import jax.numpy as jnp
import numpy as np
from jax import lax
from jax.experimental import pallas as pl
from jax.experimental.pallas import tpu as pltpu

F32 = jnp.float32
BF16 = jnp.bfloat16

D_MODEL = 1024
W = 256
N_BRANCH = 5
HEAD_D = 64
CONF_K = 31
SC_K = 3
AT_HEADS = 4
AT_KV_HEADS = 2
AT_BLOCK = 128
AT_WINDOW = 128
REL_BUCKETS = 32
REL_MAX_DIST = 128
EPS = 1e-6

(C_HQ, C_HF, C_HB, C_HI, C_HZ, C_CA, C_CB, C_CZ, C_SB, C_SC, C_SV, C_SZ,
 C_AQ, C_AKV, C_AZ, C_MQ, C_MZ) = range(17)
IN_WIDTH = 17 * W

HG_BLOCK = 16
HALO = 16
VMEM_LIMIT = 56 * 1024 * 1024


def _sigmoid(x):
    return 1.0 / (1.0 + jnp.exp(-x))


def _silu(x):
    return x * _sigmoid(x)


def _rms(x, g):
    return x * lax.rsqrt(jnp.mean(x * x, axis=-1, keepdims=True) + EPS) * g


def _params(sem):
    return pltpu.CompilerParams(dimension_semantics=sem, vmem_limit_bytes=VMEM_LIMIT)


def _inproj_kernel(x_ref, g_ref, w_ref, u_ref):
    h = _rms(x_ref[...], g_ref[...]).astype(BF16)
    u_ref[...] = jnp.dot(h, w_ref[...], preferred_element_type=F32)


def _inproj(x2, g, w, tm):
    t = x2.shape[0]
    return pl.pallas_call(
        _inproj_kernel,
        grid=(t // tm,),
        in_specs=[
            pl.BlockSpec((tm, D_MODEL), lambda i: (i, 0)),
            pl.BlockSpec((1, D_MODEL), lambda i: (0, 0)),
            pl.BlockSpec((D_MODEL, IN_WIDTH), lambda i: (0, 0)),
        ],
        out_specs=pl.BlockSpec((tm, IN_WIDTH), lambda i: (i, 0)),
        out_shape=jax.ShapeDtypeStruct((t, IN_WIDTH), F32),
        compiler_params=_params(("parallel",)),
        name="inproj",
    )(x2, g, w)


def _memkv_kernel(m_ref, g_ref, w_ref, o_ref):
    h = _rms(m_ref[...], g_ref[0]).astype(BF16)
    o_ref[0] = jnp.dot(h, w_ref[0], preferred_element_type=F32).astype(BF16)


def _memkv(mem2, g, w, tm):
    depth = w.shape[0]
    t = mem2.shape[0]
    return pl.pallas_call(
        _memkv_kernel,
        grid=(depth, t // tm),
        in_specs=[
            pl.BlockSpec((tm, D_MODEL), lambda l, i: (i, 0)),
            pl.BlockSpec((1, 1, D_MODEL), lambda l, i: (l, 0, 0)),
            pl.BlockSpec((1, D_MODEL, 2 * W), lambda l, i: (l, 0, 0)),
        ],
        out_specs=pl.BlockSpec((1, tm, 2 * W), lambda l, i: (l, i, 0)),
        out_shape=jax.ShapeDtypeStruct((depth, t, 2 * W), BF16),
        compiler_params=_params(("parallel", "parallel")),
        name="memkv",
    )(mem2, g, w)


def _split3(x):
    x1 = x.astype(BF16)
    r1 = x - x1.astype(F32)
    x2 = r1.astype(BF16)
    x3 = (r1 - x2.astype(F32)).astype(BF16)
    return x1, x2, x3


def _dot3(m, x):
    x1, x2, x3 = _split3(x)
    acc = jnp.dot(m, x1, preferred_element_type=F32)
    acc = acc + jnp.dot(m, x2, preferred_element_type=F32)
    return acc + jnp.dot(m, x3, preferred_element_type=F32)


def _hgrn_scan(q_ref, f_ref, v_ref, lb_ref, tri_ref, blk_ref, ones_ref,
               a_s, kk_s, qe_s, ke_s, o_s, st_s, reverse):
    ts = q_ref.shape[1]
    nb = ts // HG_BLOCK
    lb = lb_ref[...]
    f = lb + (1.0 - lb) * _sigmoid(f_ref[0])
    g = jnp.log(f)
    a = _dot3(tri_ref[...], g)
    a_tot = _dot3(blk_ref[...], g)
    a_s[...] = a
    kk = 1.0 - f
    kk_s[...] = kk
    qe_s[...] = q_ref[0] * jnp.exp(a)
    ke_s[...] = kk * jnp.exp(a_tot - a)

    @pl.when(pl.program_id(1) == 0)
    def _():
        st_s[...] = jnp.zeros_like(st_s)

    row = lax.broadcasted_iota(jnp.int32, (HG_BLOCK, W), 0)
    hv = lax.broadcasted_iota(jnp.int32, (W, W), 0) // HEAD_D
    hk = lax.broadcasted_iota(jnp.int32, (W, W), 1) // HEAD_D
    same_head = hv == hk
    ones_bd = ones_ref[...]

    def body(it, carry):
        blk = (nb - 1 - it) if reverse else it
        r0 = pl.multiple_of(blk * HG_BLOCK, HG_BLOCK)
        rows = pl.ds(r0, HG_BLOCK)
        a_b = a_s[rows, :]
        q_b = q_ref[0, rows, :]
        kk_b = kk_s[rows, :]
        v_b = v_ref[0, rows, :]
        pieces = []
        for s in range(HG_BLOCK):
            d = jnp.minimum(a_b - a_b[s:s + 1, :], 0.0)
            dm = q_b * kk_b[s:s + 1, :] * jnp.exp(d)
            keep = (row <= s) if reverse else (row >= s)
            pieces.append(jnp.where(keep, dm, 0.0).astype(BF16))
        d_all = jnp.concatenate(pieces, axis=0)
        r_all = jnp.dot(d_all, ones_bd, preferred_element_type=F32)
        o_b = jnp.zeros((HG_BLOCK, W), F32)
        for s in range(HG_BLOCK):
            o_b = o_b + r_all[s * HG_BLOCK:(s + 1) * HG_BLOCK, :] * v_b[s:s + 1, :]
        st = st_s[...]
        st_m = jnp.where(same_head, st, 0.0).astype(BF16)
        o_b = o_b + lax.dot_general(qe_s[rows, :].astype(BF16), st_m,
                                    (((1,), (1,)), ((), ())),
                                    preferred_element_type=F32)
        o_s[rows, :] = o_b
        upd = lax.dot_general(v_b.astype(BF16), ke_s[rows, :].astype(BF16),
                              (((0,), (0,)), ((), ())),
                              preferred_element_type=F32)
        a_end = a_b[0:1, :] if reverse else a_b[HG_BLOCK - 1:HG_BLOCK, :]
        st_s[...] = st * jnp.exp(a_end) + upd
        return carry

    lax.fori_loop(0, nb, body, 0)


def _hgrn_bwd_kernel(q_ref, f_ref, v_ref, lb_ref, tri_ref, blk_ref, ones_ref, o_ref,
                     a_s, kk_s, qe_s, ke_s, o_s, st_s):
    _hgrn_scan(q_ref, f_ref, v_ref, lb_ref, tri_ref, blk_ref, ones_ref,
               a_s, kk_s, qe_s, ke_s, o_s, st_s, reverse=True)
    o_ref[0] = o_s[...]


def _hgrn_fwd_kernel(q_ref, f_ref, v_ref, lb_ref, tri_ref, blk_ref, ones_ref,
                     ob_ref, z_ref, on_ref, y_ref,
                     a_s, kk_s, qe_s, ke_s, o_s, st_s):
    _hgrn_scan(q_ref, f_ref, v_ref, lb_ref, tri_ref, blk_ref, ones_ref,
               a_s, kk_s, qe_s, ke_s, o_s, st_s, reverse=False)
    o = o_s[...] + ob_ref[0]
    ms = _dot3_right(o * o, ones_ref[...]) * (1.0 / HEAD_D)
    on = o * lax.rsqrt(ms + EPS) * on_ref[...]
    y_ref[0] = (on * _silu(z_ref[0])).astype(BF16)


def _dot3_right(x, m):
    x1, x2, x3 = _split3(x)
    acc = jnp.dot(x1, m, preferred_element_type=F32)
    acc = acc + jnp.dot(x2, m, preferred_element_type=F32)
    return acc + jnp.dot(x3, m, preferred_element_type=F32)


def _hgrn_consts(ts):
    r = np.arange(ts)
    same_blk = (r[:, None] // HG_BLOCK) == (r[None, :] // HG_BLOCK)
    tri_f = same_blk & (r[None, :] <= r[:, None])
    tri_b = same_blk & (r[None, :] >= r[:, None])
    c = np.arange(W)
    ones_bd = (c[:, None] // HEAD_D) == (c[None, :] // HEAD_D)
    as_bf = lambda m: jnp.asarray(m.astype(np.float32), dtype=BF16)
    return as_bf(tri_f), as_bf(tri_b), as_bf(same_blk), as_bf(ones_bd)


def _hgrn_scratch(ts):
    return [pltpu.VMEM((ts, W), F32) for _ in range(5)] + [pltpu.VMEM((W, W), F32)]


def _hgrn(u3, lb_f, lb_b, onorm_t, ts):
    b, s, _ = u3.shape
    nt = s // ts
    tri_f, tri_b, blk_m, ones_bd = _hgrn_consts(ts)
    const = lambda shape: pl.BlockSpec(shape, lambda bi, i: (0,) * len(shape))

    def col(c, rev):
        if rev:
            return pl.BlockSpec((1, ts, W), lambda bi, i: (bi, nt - 1 - i, c))
        return pl.BlockSpec((1, ts, W), lambda bi, i: (bi, i, c))

    o_bwd = pl.pallas_call(
        _hgrn_bwd_kernel,
        grid=(b, nt),
        in_specs=[col(C_HQ, True), col(C_HB, True), col(C_HI, True),
                  const((1, W)), const((ts, ts)), const((ts, ts)), const((W, W))],
        out_specs=pl.BlockSpec((1, ts, W), lambda bi, i: (bi, nt - 1 - i, 0)),
        out_shape=jax.ShapeDtypeStruct((b, s, W), F32),
        scratch_shapes=_hgrn_scratch(ts),
        compiler_params=_params(("parallel", "arbitrary")),
        name="hgrn_bwd",
    )(u3, u3, u3, lb_b, tri_b, blk_m, ones_bd)

    return pl.pallas_call(
        _hgrn_fwd_kernel,
        grid=(b, nt),
        in_specs=[col(C_HQ, False), col(C_HF, False), col(C_HI, False),
                  const((1, W)), const((ts, ts)), const((ts, ts)), const((W, W)),
                  pl.BlockSpec((1, ts, W), lambda bi, i: (bi, i, 0)),
                  col(C_HZ, False), const((1, W))],
        out_specs=pl.BlockSpec((1, ts, W), lambda bi, i: (bi, i, 0)),
        out_shape=jax.ShapeDtypeStruct((b, s, W), BF16),
        scratch_shapes=_hgrn_scratch(ts),
        compiler_params=_params(("parallel", "arbitrary")),
        name="hgrn_fwd",
    )(u3, u3, u3, lb_f, tri_f, blk_m, ones_bd, o_bwd, u3, onorm_t)


CONV_ROWS = 64


def _depthwise(ext_ref, w_ref, ntap, first, ts, emit):
    for r0 in range(0, ts, CONV_ROWS):
        acc = None
        for j in range(ntap):
            term = w_ref[j:j + 1, :] * ext_ref[pl.ds(first + r0 + j, CONV_ROWS), :]
            acc = term if acc is None else acc + term
        emit(r0, acc)


def _conv_kernel(ca, cb, cz, ca_p, cb_p, ca_n, cb_n,
                 sb, sc, sv, sz, sc_p, sv_p, sc_n, sv_n,
                 wdw, bdw, lng, lnb, scw, ycf, ysc, ext_c, ext_s):
    ts = ca.shape[1]
    i = pl.program_id(1)
    has_prev = i > 0
    has_next = i < pl.num_programs(1) - 1

    ext_c[pl.ds(HALO, ts), :] = ca[0] * _sigmoid(cb[0])
    ext_c[pl.ds(0, HALO), :] = jnp.where(has_prev, ca_p[0] * _sigmoid(cb_p[0]), 0.0)
    ext_c[pl.ds(HALO + ts, HALO), :] = jnp.where(has_next, ca_n[0] * _sigmoid(cb_n[0]), 0.0)
    ext_s[pl.ds(HALO, ts), :] = sc[0] * sv[0]
    ext_s[pl.ds(0, HALO), :] = jnp.where(has_prev, sc_p[0] * sv_p[0], 0.0)
    ext_s[pl.ds(HALO + ts, HALO), :] = jnp.where(has_next, sc_n[0] * sv_n[0], 0.0)

    def emit_conf(r0, acc):
        rows = pl.ds(r0, CONV_ROWS)
        c = acc + bdw[...]
        mu = jnp.mean(c, axis=-1, keepdims=True)
        cc = c - mu
        var = jnp.mean(cc * cc, axis=-1, keepdims=True)
        n = cc * lax.rsqrt(var + EPS) * lng[...] + lnb[...]
        ycf[0, rows, :] = (_silu(n) * _silu(cz[0, rows, :])).astype(BF16)

    _depthwise(ext_c, wdw, CONF_K, HALO - (CONF_K - 1) // 2, ts, emit_conf)

    def emit_short(r0, acc):
        rows = pl.ds(r0, CONV_ROWS)
        ysc[0, rows, :] = (sb[0, rows, :] * acc * _silu(sz[0, rows, :])).astype(BF16)

    _depthwise(ext_s, scw, SC_K, HALO - (SC_K - 1) // 2, ts, emit_short)


def _conv(u3, wdw, bdw, lng, lnb, scw, ts):
    b, s, _ = u3.shape
    per = ts // HALO
    last = s // HALO - 1
    cur = lambda c: pl.BlockSpec((1, ts, W), lambda bi, i: (bi, i, c))
    prev = lambda c: pl.BlockSpec(
        (1, HALO, W), lambda bi, i: (bi, jnp.maximum(i * per - 1, 0), c))
    nxt = lambda c: pl.BlockSpec(
        (1, HALO, W), lambda bi, i: (bi, jnp.minimum((i + 1) * per, last), c))
    const = lambda shape: pl.BlockSpec(shape, lambda bi, i: (0,) * len(shape))
    out = pl.BlockSpec((1, ts, W), lambda bi, i: (bi, i, 0))
    return pl.pallas_call(
        _conv_kernel,
        grid=(b, s // ts),
        in_specs=[cur(C_CA), cur(C_CB), cur(C_CZ),
                  prev(C_CA), prev(C_CB), nxt(C_CA), nxt(C_CB),
                  cur(C_SB), cur(C_SC), cur(C_SV), cur(C_SZ),
                  prev(C_SC), prev(C_SV), nxt(C_SC), nxt(C_SV),
                  const((CONF_K, W)), const((1, W)), const((1, W)), const((1, W)),
                  const((SC_K, W))],
        out_specs=[out, out],
        out_shape=[jax.ShapeDtypeStruct((b, s, W), BF16)] * 2,
        scratch_shapes=[pltpu.VMEM((ts + 2 * HALO, W), F32)] * 2,
        compiler_params=_params(("parallel", "parallel")),
        name="conv",
    )(*([u3] * 15), wdw, bdw, lng, lnb, scw)


def _attn_kernel(aq, akv, akv_p, akv_n, az, mq, mz, mkv, bias, sink, yat, ymem, kv_ext):
    ts = aq.shape[1]
    nq = ts // AT_BLOCK
    i = pl.program_id(1)
    is_first = i == 0
    is_last = i == pl.num_programs(1) - 1
    scale = HEAD_D ** -0.5

    kv_ext[pl.ds(0, AT_BLOCK), :] = akv_p[0].astype(BF16)
    kv_ext[pl.ds(AT_BLOCK, ts), :] = akv[0].astype(BF16)
    kv_ext[pl.ds(AT_BLOCK + ts, AT_BLOCK), :] = akv_n[0].astype(BF16)

    col = lax.broadcasted_iota(jnp.int32, (AT_BLOCK, 3 * AT_BLOCK), 1)
    for j in range(nq):
        rows = pl.ds(j * AT_BLOCK, AT_BLOCK)
        band = kv_ext[pl.ds(j * AT_BLOCK, 3 * AT_BLOCK), :]
        off_seq = None
        if j == 0:
            off_seq = jnp.logical_and(is_first, col < AT_BLOCK)
        if j == nq - 1:
            hi = jnp.logical_and(is_last, col >= 2 * AT_BLOCK)
            off_seq = hi if off_seq is None else jnp.logical_or(off_seq, hi)
        q_blk = aq[0, rows, :] * scale
        outs = []
        for h in range(AT_HEADS):
            kh = h // (AT_HEADS // AT_KV_HEADS)
            q_h = q_blk[:, h * HEAD_D:(h + 1) * HEAD_D].astype(BF16)
            k_h = band[:, kh * HEAD_D:(kh + 1) * HEAD_D]
            v_h = band[:, (AT_KV_HEADS + kh) * HEAD_D:(AT_KV_HEADS + kh + 1) * HEAD_D]
            lg = lax.dot_general(q_h, k_h, (((1,), (1,)), ((), ())),
                                 preferred_element_type=F32) + bias[h]
            if off_seq is not None:
                lg = jnp.where(off_seq, -jnp.inf, lg)
            sk = sink[h]
            m = jnp.maximum(jnp.max(lg, axis=-1, keepdims=True), sk)
            e = jnp.exp(lg - m)
            den = jnp.sum(e, axis=-1, keepdims=True) + jnp.exp(sk - m)
            pv = jnp.dot(e.astype(BF16), v_h, preferred_element_type=F32)
            outs.append(pv / den)
        o = jnp.concatenate(outs, axis=-1)
        yat[0, rows, :] = (o * _silu(az[0, rows, :])).astype(BF16)

    q_m = mq[0] * scale
    kv_m = mkv[0]
    outs = []
    for h in range(AT_HEADS):
        q_h = q_m[:, h * HEAD_D:(h + 1) * HEAD_D].astype(BF16)
        k_h = kv_m[:, h * HEAD_D:(h + 1) * HEAD_D]
        v_h = kv_m[:, W + h * HEAD_D:W + (h + 1) * HEAD_D]
        lg = lax.dot_general(q_h, k_h, (((1,), (1,)), ((), ())),
                             preferred_element_type=F32)
        m = jnp.max(lg, axis=-1, keepdims=True)
        e = jnp.exp(lg - m)
        den = jnp.sum(e, axis=-1, keepdims=True)
        pv = jnp.dot(e.astype(BF16), v_h, preferred_element_type=F32)
        outs.append(pv / den)
    o = jnp.concatenate(outs, axis=-1)
    ymem[0] = (o * _silu(mz[0])).astype(BF16)


def _attn(u3, mkv_l, bias, sink, ts):
    b, s, _ = u3.shape
    per = ts // AT_BLOCK
    last = s // AT_BLOCK - 1
    m = mkv_l.shape[1]
    cur = lambda c: pl.BlockSpec((1, ts, W), lambda bi, i: (bi, i, c))
    out = pl.BlockSpec((1, ts, W), lambda bi, i: (bi, i, 0))
    return pl.pallas_call(
        _attn_kernel,
        grid=(b, s // ts),
        in_specs=[cur(C_AQ), cur(C_AKV),
                  pl.BlockSpec((1, AT_BLOCK, W),
                               lambda bi, i: (bi, jnp.maximum(i * per - 1, 0), C_AKV)),
                  pl.BlockSpec((1, AT_BLOCK, W),
                               lambda bi, i: (bi, jnp.minimum((i + 1) * per, last), C_AKV)),
                  cur(C_AZ), cur(C_MQ), cur(C_MZ),
                  pl.BlockSpec((1, m, 2 * W), lambda bi, i: (bi, 0, 0)),
                  pl.BlockSpec((AT_HEADS, AT_BLOCK, 3 * AT_BLOCK), lambda bi, i: (0, 0, 0)),
                  pl.BlockSpec(memory_space=pltpu.SMEM)],
        out_specs=[out, out],
        out_shape=[jax.ShapeDtypeStruct((b, s, W), BF16)] * 2,
        scratch_shapes=[pltpu.VMEM((ts + 2 * AT_BLOCK, W), BF16)],
        compiler_params=_params(("parallel", "parallel")),
        name="attn",
    )(u3, u3, u3, u3, u3, u3, u3, mkv_l, bias, sink)


def _merge_kernel(x_ref, gpre_ref, y0, y1, y2, y3, y4, wg_ref, wb_ref, wo_ref, gpost_ref,
                  o_ref):
    x = x_ref[...]
    h = _rms(x, gpre_ref[...]).astype(BF16)
    merged = None
    for n, y_ref in enumerate((y0, y1, y2, y3, y4)):
        gate = _sigmoid(jnp.dot(h, wg_ref[n], preferred_element_type=F32))
        term = gate * jnp.dot(y_ref[...], wb_ref[n], preferred_element_type=F32)
        merged = term if merged is None else merged + term
    z = jnp.dot(merged.astype(BF16), wo_ref[...], preferred_element_type=F32)
    o_ref[...] = x + _rms(z, gpost_ref[...])


def _merge(x2, gpre, ys, wg, wb, wo, gpost, tm):
    t = x2.shape[0]
    row = lambda width: pl.BlockSpec((tm, width), lambda i: (i, 0))
    const = lambda shape: pl.BlockSpec(shape, lambda i: (0,) * len(shape),
                                       pipeline_mode=pl.Buffered(1))
    return pl.pallas_call(
        _merge_kernel,
        grid=(t // tm,),
        in_specs=[row(D_MODEL), const((1, D_MODEL))] + [row(W)] * N_BRANCH + [
            const((N_BRANCH, D_MODEL, D_MODEL)), const((N_BRANCH, W, D_MODEL)),
            const((D_MODEL, D_MODEL)), const((1, D_MODEL))],
        out_specs=row(D_MODEL),
        out_shape=jax.ShapeDtypeStruct((t, D_MODEL), F32),
        compiler_params=_params(("parallel",)),
        name="merge",
    )(x2, gpre, *ys, wg, wb, wo, gpost)


def _t5_bucket(rel):
    half = REL_BUCKETS // 2
    n = -rel
    ret = jnp.where(n < 0, half, 0)
    n = jnp.abs(n)
    max_exact = half // 2
    large = max_exact + (jnp.log(jnp.maximum(n, 1).astype(F32) / max_exact)
                         / math.log(REL_MAX_DIST / max_exact)
                         * (half - max_exact)).astype(jnp.int32)
    large = jnp.minimum(large, half - 1)
    return ret + jnp.where(n < max_exact, n, large)


def _banded_bias(rel_bias):
    qi = jnp.arange(AT_BLOCK)[:, None]
    sj = jnp.arange(3 * AT_BLOCK)[None, :]
    rel = (sj - AT_BLOCK) - qi
    bias = jnp.transpose(rel_bias[_t5_bucket(rel)], (2, 0, 1)).astype(F32)
    return jnp.where((jnp.abs(rel) <= AT_WINDOW)[None], bias, -jnp.inf)


def kernel(x, mem, norm_pre, norm_post, w_in, hg_lb_logits, hg_onorm, conf_dw_w, conf_dw_b,
           conf_ln_g, conf_ln_b, sc_w, attn_sink, rel_bias, mem_norm, w_mem_kv, w_gate,
           w_branch, w_out):
    b, s, d = x.shape
    depth = w_in.shape[0]
    assert d == D_MODEL and w_in.shape[2] == IN_WIDTH
    t = b * s
    tm = min(256, t)
    ts_hg = min(256, s)
    ts_cv = min(512, s)
    ts_at = min(512, s)

    lb_soft = jax.nn.softmax(hg_lb_logits.astype(F32), axis=0)
    lower = jnp.cumsum(lb_soft, axis=0) - lb_soft[0]
    bias = _banded_bias(rel_bias)
    onorm_t = jnp.tile(hg_onorm.astype(F32), (1, W // HEAD_D))

    w_in_b = w_in.astype(BF16)
    w_gate_b = w_gate.astype(BF16)
    w_branch_b = w_branch.astype(BF16)
    w_out_b = w_out.astype(BF16)

    m = mem.shape[1]
    mkv = _memkv(mem.reshape(b * m, d), mem_norm.reshape(depth, 1, d),
                 w_mem_kv.astype(BF16), min(256, b * m)).reshape(depth, b, m, 2 * W)

    x2 = x.reshape(t, d)
    row = lambda v: v.reshape(1, -1)
    for l in range(depth):
        u3 = _inproj(x2, row(norm_pre[l]), w_in_b[l], tm).reshape(b, s, IN_WIDTH)
        y_hg = _hgrn(u3, row(lower[l, 0]), row(lower[l, 1]), row(onorm_t[l]), ts_hg)
        y_cf, y_sc = _conv(u3, conf_dw_w[l], row(conf_dw_b[l]), row(conf_ln_g[l]),
                           row(conf_ln_b[l]), sc_w[l], ts_cv)
        y_at, y_mem = _attn(u3, mkv[l], bias, attn_sink[l], ts_at)
        ys = [y.reshape(t, W) for y in (y_hg, y_cf, y_sc, y_at, y_mem)]
        x2 = _merge(x2, row(norm_pre[l]), ys, w_gate_b[l], w_branch_b[l], w_out_b[l],
                    row(norm_post[l]), tm)
    return x2.reshape(b, s, d)
```

```python
import math

import jax
import jax.numpy as jnp
import numpy as np
from jax import lax
from jax.experimental import pallas as pl
from jax.experimental.pallas import tpu as pltpu

F32 = jnp.float32
BF16 = jnp.bfloat16

D_MODEL = 1024
W = 256
N_BRANCH = 5
HEAD_D = 64
N_HEADS = W // HEAD_D
CONF_K = 31
SC_K = 3
AT_HEADS = 4
AT_KV_HEADS = 2
AT_BLOCK = 128
AT_WINDOW = 128
REL_BUCKETS = 32
REL_MAX_DIST = 128
EPS = 1e-6

(C_HQ, C_HF, C_HB, C_HI, C_HZ, C_CA, C_CB, C_CZ, C_SB, C_SC, C_SV, C_SZ,
 C_AQ, C_AKV, C_AZ, C_MQ, C_MZ) = range(17)
IN_WIDTH = 17 * W

HALO = 16
VMEM_LIMIT = 56 * 1024 * 1024

_NT = (((1,), (1,)), ((), ()))
_TN = (((0,), (0,)), ((), ()))


def _sigmoid(x):
    return 1.0 / (1.0 + jnp.exp(-x))


def _silu(x):
    return x * _sigmoid(x)


def _rms(x, g):
    return x * lax.rsqrt(jnp.mean(x * x, axis=-1, keepdims=True) + EPS) * g


def _params(sem):
    return pltpu.CompilerParams(dimension_semantics=sem, vmem_limit_bytes=VMEM_LIMIT)


def _inproj_kernel(x_ref, g_ref, w_ref, u_ref):
    h = _rms(x_ref[...], g_ref[...]).astype(BF16)
    u_ref[...] = jnp.dot(h, w_ref[...], preferred_element_type=F32)


def _inproj(x2, g, w, tm):
    t = x2.shape[0]
    return pl.pallas_call(
        _inproj_kernel,
        grid=(t // tm,),
        in_specs=[
            pl.BlockSpec((tm, D_MODEL), lambda i: (i, 0)),
            pl.BlockSpec((1, D_MODEL), lambda i: (0, 0)),
            pl.BlockSpec((D_MODEL, IN_WIDTH), lambda i: (0, 0)),
        ],
        out_specs=pl.BlockSpec((tm, IN_WIDTH), lambda i: (i, 0)),
        out_shape=jax.ShapeDtypeStruct((t, IN_WIDTH), F32),
        compiler_params=_params(("parallel",)),
        name="inproj",
    )(x2, g, w)


def _memkv_kernel(m_ref, g_ref, w_ref, o_ref):
    h = _rms(m_ref[...], g_ref[0]).astype(BF16)
    o_ref[0] = jnp.dot(h, w_ref[0], preferred_element_type=F32).astype(BF16)


def _memkv(mem2, g, w, tm):
    depth = w.shape[0]
    t = mem2.shape[0]
    return pl.pallas_call(
        _memkv_kernel,
        grid=(depth, t // tm),
        in_specs=[
            pl.BlockSpec((tm, D_MODEL), lambda l, i: (i, 0)),
            pl.BlockSpec((1, 1, D_MODEL), lambda l, i: (l, 0, 0)),
            pl.BlockSpec((1, D_MODEL, 2 * W), lambda l, i: (l, 0, 0)),
        ],
        out_specs=pl.BlockSpec((1, tm, 2 * W), lambda l, i: (l, i, 0)),
        out_shape=jax.ShapeDtypeStruct((depth, t, 2 * W), BF16),
        compiler_params=_params(("parallel", "parallel")),
        name="memkv",
    )(mem2, g, w)


HG_CHUNK = 64
HG_LEVELS = (1, 2, 4, 8, 16, 32)


def _split3(x):
    x1 = x.astype(BF16)
    r1 = x - x1.astype(F32)
    x2 = r1.astype(BF16)
    x3 = (r1 - x2.astype(F32)).astype(BF16)
    return x1, x2, x3


def _dot3(m, x):
    x1, x2, x3 = _split3(x)
    acc = jnp.dot(m, x1, preferred_element_type=F32)
    acc = acc + jnp.dot(m, x2, preferred_element_type=F32)
    return acc + jnp.dot(m, x3, preferred_element_type=F32)


def _dot3_right(x, m):
    x1, x2, x3 = _split3(x)
    acc = jnp.dot(x1, m, preferred_element_type=F32)
    acc = acc + jnp.dot(x2, m, preferred_element_type=F32)
    return acc + jnp.dot(x3, m, preferred_element_type=F32)


def _hgrn_consts(reverse):
    c_ = HG_CHUNK
    r = np.arange(c_)[:, None]
    c = np.arange(c_)[None, :]
    masks = []
    for m in HG_LEVELS:
        same = (r // (2 * m)) == (c // (2 * m))
        masks.append(same & (r % (2 * m) >= m) & (c % (2 * m) < m))
    masks.append(r == c)
    tri = c <= r
    if reverse:
        tri = tri[::-1, ::-1]
        masks = [m[::-1, ::-1] for m in masks]
    pair = np.stack([np.tile(m, (1, N_HEADS)) for m in masks]).astype(np.float32)
    lane_head = np.arange(W) // HEAD_D
    head = np.stack([np.broadcast_to(lane_head == h, (c_, W)) for h in range(N_HEADS)])
    return (jnp.asarray(tri.astype(np.float32), dtype=BF16), jnp.asarray(pair),
            jnp.asarray(head.astype(np.float32), dtype=BF16))


def _is_query_row(m, reverse, row):
    second = (row // m) % 2 == 1
    return jnp.logical_not(second) if reverse else second


def _level_operand(q, kk, m, reverse, row):
    if m >= 8:
        parts = []
        for base in range(0, HG_CHUNK, m):
            second = (base // m) % 2 == 1
            parts.append((q if second != reverse else kk)[base:base + m])
        return jnp.concatenate(parts, axis=0)
    return jnp.where(_is_query_row(m, reverse, row), q, kk)


def _bcast_row(a, r, n):
    return jnp.broadcast_to(a[r:r + 1, :], (n, a.shape[1]))


def _level_decay(a, m, reverse, row):
    ref = m if reverse else m - 1
    if m >= 4:
        mid = jnp.concatenate([_bcast_row(a, base + ref, 2 * m)
                               for base in range(0, HG_CHUNK, 2 * m)], axis=0)
    else:
        low = lax.broadcasted_iota(jnp.int32, (8, a.shape[1]), 0) < 4
        mid = jnp.concatenate(
            [jnp.where(low, _bcast_row(a, base + ref, 8), _bcast_row(a, base + 4 + ref, 8))
             for base in range(0, HG_CHUNK, 8)], axis=0)
    return jnp.exp(-jnp.abs(a - mid))


def _head_stack(xb, head_ref):
    return jnp.concatenate([xb * head_ref[h] for h in range(N_HEADS)], axis=0)


def _hgrn_scan(q_ref, f_ref, v_ref, lb_ref, tri_ref, pair_ref, head_ref,
               g_s, fg_s, o_s, st_s, reverse):
    ts = q_ref.shape[1]
    nc = ts // HG_CHUNK
    lb = lb_ref[...]
    f = lb + (1.0 - lb) * _sigmoid(f_ref[0])
    g_s[...] = jnp.log(f)
    fg_s[...] = f

    @pl.when(pl.program_id(1) == 0)
    def _():
        st_s[...] = jnp.zeros_like(st_s)

    row = lax.broadcasted_iota(jnp.int32, (HG_CHUNK, W), 0)
    same_head = (lax.broadcasted_iota(jnp.int32, (W, W), 0) // HEAD_D
                 == lax.broadcasted_iota(jnp.int32, (W, W), 1) // HEAD_D)
    nl = len(HG_LEVELS)

    def body(it, carry):
        ci = (nc - 1 - it) if reverse else it
        rows = pl.ds(pl.multiple_of(ci * HG_CHUNK, HG_CHUNK), HG_CHUNK)
        q_c = q_ref[0, rows, :]
        f_c = fg_s[rows, :]
        kk_c = 1.0 - f_c
        v_b = v_ref[0, rows, :].astype(BF16)
        a = _dot3(tri_ref[...], g_s[rows, :])
        p = None
        for li, m in enumerate(HG_LEVELS):
            if m == 1:
                z = jnp.where(_is_query_row(1, reverse, row), q_c * f_c, kk_c)
            else:
                z = _level_decay(a, m, reverse, row) * _level_operand(q_c, kk_c, m, reverse, row)
            zb = z.astype(BF16)
            sc = lax.dot_general(zb, _head_stack(zb, head_ref), _NT,
                                 preferred_element_type=F32)
            term = pair_ref[li] * sc
            p = term if p is None else p + term
        sc = lax.dot_general(q_c.astype(BF16), _head_stack(kk_c.astype(BF16), head_ref),
                             _NT, preferred_element_type=F32)
        p = p + pair_ref[nl] * sc
        o = jnp.dot(p.astype(BF16), _head_stack(v_b, head_ref),
                    preferred_element_type=F32)
        st = st_s[...]
        st_m = jnp.where(same_head, st, 0.0).astype(BF16)
        qe = (q_c * jnp.exp(a)).astype(BF16)
        o = o + lax.dot_general(qe, st_m, _NT, preferred_element_type=F32)
        o_s[rows, :] = o
        edge = 0 if reverse else HG_CHUNK - 1
        a_tot = a[edge:edge + 1, :]
        ke = (kk_c * jnp.exp(a_tot - a)).astype(BF16)
        upd = lax.dot_general(v_b, ke, _TN, preferred_element_type=F32)
        st_s[...] = st * jnp.exp(a_tot) + upd
        return carry

    lax.fori_loop(0, nc, body, 0, unroll=True)


def _hgrn_bwd_kernel(q_ref, f_ref, v_ref, lb_ref, tri_ref, pair_ref, head_ref, o_ref,
                     g_s, fg_s, o_s, st_s):
    _hgrn_scan(q_ref, f_ref, v_ref, lb_ref, tri_ref, pair_ref, head_ref,
               g_s, fg_s, o_s, st_s, reverse=True)
    o_ref[0] = o_s[...]


def _hgrn_fwd_kernel(q_ref, f_ref, v_ref, lb_ref, tri_ref, pair_ref, head_ref,
                     ob_ref, z_ref, on_ref, ones_ref, y_ref, g_s, fg_s, o_s, st_s):
    _hgrn_scan(q_ref, f_ref, v_ref, lb_ref, tri_ref, pair_ref, head_ref,
               g_s, fg_s, o_s, st_s, reverse=False)
    o = o_s[...] + ob_ref[0]
    ms = _dot3_right(o * o, ones_ref[...]) * (1.0 / HEAD_D)
    on = o * lax.rsqrt(ms + EPS) * on_ref[...]
    y_ref[0] = (on * _silu(z_ref[0])).astype(BF16)


def _hgrn_scratch(ts):
    return [pltpu.VMEM((ts, W), F32) for _ in range(3)] + [pltpu.VMEM((W, W), F32)]


def _hgrn(u3, lb_f, lb_b, onorm_t, ts):
    b, s, _ = u3.shape
    nt = s // ts
    lane_head = np.arange(W) // HEAD_D
    ones_bd = jnp.asarray((lane_head[:, None] == lane_head[None, :]).astype(np.float32),
                          dtype=BF16)
    const = lambda shape: pl.BlockSpec(shape, lambda bi, i: (0,) * len(shape))
    const_specs = [const((1, W)), const((HG_CHUNK, HG_CHUNK)),
                   const((len(HG_LEVELS) + 1, HG_CHUNK, W)), const((N_HEADS, HG_CHUNK, W))]

    def col(c, rev):
        if rev:
            return pl.BlockSpec((1, ts, W), lambda bi, i: (bi, nt - 1 - i, c))
        return pl.BlockSpec((1, ts, W), lambda bi, i: (bi, i, c))

    o_bwd = pl.pallas_call(
        _hgrn_bwd_kernel,
        grid=(b, nt),
        in_specs=[col(C_HQ, True), col(C_HB, True), col(C_HI, True)] + const_specs,
        out_specs=pl.BlockSpec((1, ts, W), lambda bi, i: (bi, nt - 1 - i, 0)),
        out_shape=jax.ShapeDtypeStruct((b, s, W), F32),
        scratch_shapes=_hgrn_scratch(ts),
        compiler_params=_params(("parallel", "arbitrary")),
        name="hgrn_bwd",
    )(u3, u3, u3, lb_b, *_hgrn_consts(True))

    return pl.pallas_call(
        _hgrn_fwd_kernel,
        grid=(b, nt),
        in_specs=[col(C_HQ, False), col(C_HF, False), col(C_HI, False)] + const_specs + [
            pl.BlockSpec((1, ts, W), lambda bi, i: (bi, i, 0)),
            col(C_HZ, False), const((1, W)), const((W, W))],
        out_specs=pl.BlockSpec((1, ts, W), lambda bi, i: (bi, i, 0)),
        out_shape=jax.ShapeDtypeStruct((b, s, W), BF16),
        scratch_shapes=_hgrn_scratch(ts),
        compiler_params=_params(("parallel", "arbitrary")),
        name="hgrn_fwd",
    )(u3, u3, u3, lb_f, *_hgrn_consts(False), o_bwd, u3, onorm_t, ones_bd)


CONV_ROWS = 64


def _depthwise(ext_ref, w_ref, ntap, first, ts, emit):
    for r0 in range(0, ts, CONV_ROWS):
        acc = None
        for j in range(ntap):
            term = w_ref[j:j + 1, :] * ext_ref[pl.ds(first + r0 + j, CONV_ROWS), :]
            acc = term if acc is None else acc + term
        emit(r0, acc)


def _conv_kernel(ca, cb, cz, ca_p, cb_p, ca_n, cb_n,
                 sb, sc, sv, sz, sc_p, sv_p, sc_n, sv_n,
                 wdw, bdw, lng, lnb, scw, ycf, ysc, ext_c, ext_s):
    ts = ca.shape[1]
    i = pl.program_id(1)
    has_prev = i > 0
    has_next = i < pl.num_programs(1) - 1

    ext_c[pl.ds(HALO, ts), :] = ca[0] * _sigmoid(cb[0])
    ext_c[pl.ds(0, HALO), :] = jnp.where(has_prev, ca_p[0] * _sigmoid(cb_p[0]), 0.0)
    ext_c[pl.ds(HALO + ts, HALO), :] = jnp.where(has_next, ca_n[0] * _sigmoid(cb_n[0]), 0.0)
    ext_s[pl.ds(HALO, ts), :] = sc[0] * sv[0]
    ext_s[pl.ds(0, HALO), :] = jnp.where(has_prev, sc_p[0] * sv_p[0], 0.0)
    ext_s[pl.ds(HALO + ts, HALO), :] = jnp.where(has_next, sc_n[0] * sv_n[0], 0.0)

    def emit_conf(r0, acc):
        rows = pl.ds(r0, CONV_ROWS)
        c = acc + bdw[...]
        mu = jnp.mean(c, axis=-1, keepdims=True)
        cc = c - mu
        var = jnp.mean(cc * cc, axis=-1, keepdims=True)
        n = cc * lax.rsqrt(var + EPS) * lng[...] + lnb[...]
        ycf[0, rows, :] = (_silu(n) * _silu(cz[0, rows, :])).astype(BF16)

    _depthwise(ext_c, wdw, CONF_K, HALO - (CONF_K - 1) // 2, ts, emit_conf)

    def emit_short(r0, acc):
        rows = pl.ds(r0, CONV_ROWS)
        ysc[0, rows, :] = (sb[0, rows, :] * acc * _silu(sz[0, rows, :])).astype(BF16)

    _depthwise(ext_s, scw, SC_K, HALO - (SC_K - 1) // 2, ts, emit_short)


def _conv(u3, wdw, bdw, lng, lnb, scw, ts):
    b, s, _ = u3.shape
    per = ts // HALO
    last = s // HALO - 1
    cur = lambda c: pl.BlockSpec((1, ts, W), lambda bi, i: (bi, i, c))
    prev = lambda c: pl.BlockSpec(
        (1, HALO, W), lambda bi, i: (bi, jnp.maximum(i * per - 1, 0), c))
    nxt = lambda c: pl.BlockSpec(
        (1, HALO, W), lambda bi, i: (bi, jnp.minimum((i + 1) * per, last), c))
    const = lambda shape: pl.BlockSpec(shape, lambda bi, i: (0,) * len(shape))
    out = pl.BlockSpec((1, ts, W), lambda bi, i: (bi, i, 0))
    return pl.pallas_call(
        _conv_kernel,
        grid=(b, s // ts),
        in_specs=[cur(C_CA), cur(C_CB), cur(C_CZ),
                  prev(C_CA), prev(C_CB), nxt(C_CA), nxt(C_CB),
                  cur(C_SB), cur(C_SC), cur(C_SV), cur(C_SZ),
                  prev(C_SC), prev(C_SV), nxt(C_SC), nxt(C_SV),
                  const((CONF_K, W)), const((1, W)), const((1, W)), const((1, W)),
                  const((SC_K, W))],
        out_specs=[out, out],
        out_shape=[jax.ShapeDtypeStruct((b, s, W), BF16)] * 2,
        scratch_shapes=[pltpu.VMEM((ts + 2 * HALO, W), F32)] * 2,
        compiler_params=_params(("parallel", "parallel")),
        name="conv",
    )(*([u3] * 15), wdw, bdw, lng, lnb, scw)


def _attn_kernel(aq, akv, akv_p, akv_n, az, mq, mz, mkv, bias, sink, yat, ymem, kv_ext):
    ts = aq.shape[1]
    nq = ts // AT_BLOCK
    i = pl.program_id(1)
    is_first = i == 0
    is_last = i == pl.num_programs(1) - 1
    scale = HEAD_D ** -0.5

    kv_ext[pl.ds(0, AT_BLOCK), :] = akv_p[0].astype(BF16)
    kv_ext[pl.ds(AT_BLOCK, ts), :] = akv[0].astype(BF16)
    kv_ext[pl.ds(AT_BLOCK + ts, AT_BLOCK), :] = akv_n[0].astype(BF16)

    col = lax.broadcasted_iota(jnp.int32, (AT_BLOCK, 3 * AT_BLOCK), 1)
    for j in range(nq):
        rows = pl.ds(j * AT_BLOCK, AT_BLOCK)
        band = kv_ext[pl.ds(j * AT_BLOCK, 3 * AT_BLOCK), :]
        off_seq = None
        if j == 0:
            off_seq = jnp.logical_and(is_first, col < AT_BLOCK)
        if j == nq - 1:
            hi = jnp.logical_and(is_last, col >= 2 * AT_BLOCK)
            off_seq = hi if off_seq is None else jnp.logical_or(off_seq, hi)
        q_blk = aq[0, rows, :] * scale
        outs = []
        for h in range(AT_HEADS):
            kh = h // (AT_HEADS // AT_KV_HEADS)
            q_h = q_blk[:, h * HEAD_D:(h + 1) * HEAD_D].astype(BF16)
            k_h = band[:, kh * HEAD_D:(kh + 1) * HEAD_D]
            v_h = band[:, (AT_KV_HEADS + kh) * HEAD_D:(AT_KV_HEADS + kh + 1) * HEAD_D]
            lg = lax.dot_general(q_h, k_h, _NT, preferred_element_type=F32) + bias[h]
            if off_seq is not None:
                lg = jnp.where(off_seq, -jnp.inf, lg)
            sk = sink[h]
            m = jnp.maximum(jnp.max(lg, axis=-1, keepdims=True), sk)
            e = jnp.exp(lg - m)
            den = jnp.sum(e, axis=-1, keepdims=True) + jnp.exp(sk - m)
            pv = jnp.dot(e.astype(BF16), v_h, preferred_element_type=F32)
            outs.append(pv / den)
        o = jnp.concatenate(outs, axis=-1)
        yat[0, rows, :] = (o * _silu(az[0, rows, :])).astype(BF16)

    q_m = mq[0] * scale
    kv_m = mkv[0]
    outs = []
    for h in range(AT_HEADS):
        q_h = q_m[:, h * HEAD_D:(h + 1) * HEAD_D].astype(BF16)
        k_h = kv_m[:, h * HEAD_D:(h + 1) * HEAD_D]
        v_h = kv_m[:, W + h * HEAD_D:W + (h + 1) * HEAD_D]
        lg = lax.dot_general(q_h, k_h, _NT, preferred_element_type=F32)
        m = jnp.max(lg, axis=-1, keepdims=True)
        e = jnp.exp(lg - m)
        den = jnp.sum(e, axis=-1, keepdims=True)
        pv = jnp.dot(e.astype(BF16), v_h, preferred_element_type=F32)
        outs.append(pv / den)
    o = jnp.concatenate(outs, axis=-1)
    ymem[0] = (o * _silu(mz[0])).astype(BF16)


def _attn(u3, mkv_l, bias, sink, ts):
    b, s, _ = u3.shape
    per = ts // AT_BLOCK
    last = s // AT_BLOCK - 1
    m = mkv_l.shape[1]
    cur = lambda c: pl.BlockSpec((1, ts, W), lambda bi, i: (bi, i, c))
    out = pl.BlockSpec((1, ts, W), lambda bi, i: (bi, i, 0))
    return pl.pallas_call(
        _attn_kernel,
        grid=(b, s // ts),
        in_specs=[cur(C_AQ), cur(C_AKV),
                  pl.BlockSpec((1, AT_BLOCK, W),
                               lambda bi, i: (bi, jnp.maximum(i * per - 1, 0), C_AKV)),
                  pl.BlockSpec((1, AT_BLOCK, W),
                               lambda bi, i: (bi, jnp.minimum((i + 1) * per, last), C_AKV)),
                  cur(C_AZ), cur(C_MQ), cur(C_MZ),
                  pl.BlockSpec((1, m, 2 * W), lambda bi, i: (bi, 0, 0)),
                  pl.BlockSpec((AT_HEADS, AT_BLOCK, 3 * AT_BLOCK), lambda bi, i: (0, 0, 0)),
                  pl.BlockSpec(memory_space=pltpu.SMEM)],
        out_specs=[out, out],
        out_shape=[jax.ShapeDtypeStruct((b, s, W), BF16)] * 2,
        scratch_shapes=[pltpu.VMEM((ts + 2 * AT_BLOCK, W), BF16)],
        compiler_params=_params(("parallel", "parallel")),
        name="attn",
    )(u3, u3, u3, u3, u3, u3, u3, mkv_l, bias, sink)


def _merge_kernel(x_ref, gpre_ref, y0, y1, y2, y3, y4, wg_ref, wb_ref, wo_ref, gpost_ref,
                  o_ref):
    x = x_ref[...]
    h = _rms(x, gpre_ref[...]).astype(BF16)
    merged = None
    for n, y_ref in enumerate((y0, y1, y2, y3, y4)):
        gate = _sigmoid(jnp.dot(h, wg_ref[n], preferred_element_type=F32))
        term = gate * jnp.dot(y_ref[...], wb_ref[n], preferred_element_type=F32)
        merged = term if merged is None else merged + term
    z = jnp.dot(merged.astype(BF16), wo_ref[...], preferred_element_type=F32)
    o_ref[...] = x + _rms(z, gpost_ref[...])


def _merge(x2, gpre, ys, wg, wb, wo, gpost, tm):
    t = x2.shape[0]
    row = lambda width: pl.BlockSpec((tm, width), lambda i: (i, 0))
    const = lambda shape: pl.BlockSpec(shape, lambda i: (0,) * len(shape),
                                       pipeline_mode=pl.Buffered(1))
    return pl.pallas_call(
        _merge_kernel,
        grid=(t // tm,),
        in_specs=[row(D_MODEL), const((1, D_MODEL))] + [row(W)] * N_BRANCH + [
            const((N_BRANCH, D_MODEL, D_MODEL)), const((N_BRANCH, W, D_MODEL)),
            const((D_MODEL, D_MODEL)), const((1, D_MODEL))],
        out_specs=row(D_MODEL),
        out_shape=jax.ShapeDtypeStruct((t, D_MODEL), F32),
        compiler_params=_params(("parallel",)),
        name="merge",
    )(x2, gpre, *ys, wg, wb, wo, gpost)


def _t5_bucket(rel):
    half = REL_BUCKETS // 2
    n = -rel
    ret = jnp.where(n < 0, half, 0)
    n = jnp.abs(n)
    max_exact = half // 2
    large = max_exact + (jnp.log(jnp.maximum(n, 1).astype(F32) / max_exact)
                         / math.log(REL_MAX_DIST / max_exact)
                         * (half - max_exact)).astype(jnp.int32)
    large = jnp.minimum(large, half - 1)
    return ret + jnp.where(n < max_exact, n, large)


def _banded_bias(rel_bias):
    qi = jnp.arange(AT_BLOCK)[:, None]
    sj = jnp.arange(3 * AT_BLOCK)[None, :]
    rel = (sj - AT_BLOCK) - qi
    onehot = (_t5_bucket(rel)[..., None] == jnp.arange(REL_BUCKETS)).astype(F32)
    bias = jnp.einsum("qsb,bh->hqs", onehot, rel_bias.astype(F32),
                      precision=lax.Precision.HIGHEST)
    return jnp.where((jnp.abs(rel) <= AT_WINDOW)[None], bias, -jnp.inf)


def kernel(x, mem, norm_pre, norm_post, w_in, hg_lb_logits, hg_onorm, conf_dw_w, conf_dw_b,
           conf_ln_g, conf_ln_b, sc_w, attn_sink, rel_bias, mem_norm, w_mem_kv, w_gate,
           w_branch, w_out):
    b, s, d = x.shape
    depth = w_in.shape[0]
    assert d == D_MODEL and w_in.shape[2] == IN_WIDTH
    t = b * s
    tm = min(256, t)
    ts_hg = min(512, s)
    ts_cv = min(512, s)
    ts_at = min(512, s)

    lb_soft = jax.nn.softmax(hg_lb_logits.astype(F32), axis=0)
    lower = jnp.cumsum(lb_soft, axis=0) - lb_soft[0]
    bias = _banded_bias(rel_bias)
    onorm_t = jnp.tile(hg_onorm.astype(F32), (1, N_HEADS))

    w_in_b = w_in.astype(BF16)
    w_gate_b = w_gate.astype(BF16)
    w_branch_b = w_branch.astype(BF16)
    w_out_b = w_out.astype(BF16)

    m = mem.shape[1]
    mkv = _memkv(mem.reshape(b * m, d), mem_norm.reshape(depth, 1, d),
                 w_mem_kv.astype(BF16), min(256, b * m)).reshape(depth, b, m, 2 * W)

    x2 = x.reshape(t, d)
    row = lambda v: v.reshape(1, -1)
    for l in range(depth):
        u3 = _inproj(x2, row(norm_pre[l]), w_in_b[l], tm).reshape(b, s, IN_WIDTH)
        y_hg = _hgrn(u3, row(lower[l, 0]), row(lower[l, 1]), row(onorm_t[l]), ts_hg)
        y_cf, y_sc = _conv(u3, conf_dw_w[l], row(conf_dw_b[l]), row(conf_ln_g[l]),
                           row(conf_ln_b[l]), sc_w[l], ts_cv)
        y_at, y_mem = _attn(u3, mkv[l], bias, attn_sink[l], ts_at)
        ys = [y.reshape(t, W) for y in (y_hg, y_cf, y_sc, y_at, y_mem)]
        x2 = _merge(x2, row(norm_pre[l]), ys, w_gate_b[l], w_branch_b[l], w_out_b[l],
                    row(norm_post[l]), tm)
    return x2.reshape(b, s, d)
```

```python
import math

import jax
import jax.numpy as jnp
import numpy as np
from jax import lax
from jax.experimental import pallas as pl
from jax.experimental.pallas import tpu as pltpu

F32 = jnp.float32
BF16 = jnp.bfloat16

D_MODEL = 1024
W = 256
N_BRANCH = 5
HEAD_D = 64
N_HEADS = W // HEAD_D
CONF_K = 31
SC_K = 3
AT_HEADS = 4
AT_KV_HEADS = 2
AT_BLOCK = 128
AT_WINDOW = 128
REL_BUCKETS = 32
REL_MAX_DIST = 128
EPS = 1e-6
LOG2E = math.log2(math.e)

G_HG, G_CV, G_AT = 5 * W, 7 * W, 5 * W
IN_WIDTH = G_HG + G_CV + G_AT
HG_Q, HG_FF, HG_FB, HG_I, HG_Z = (n * W for n in range(5))
CV_A, CV_B, CV_Z, CV_SB, CV_SC, CV_SV, CV_SZ = (n * W for n in range(7))
AT_Q, AT_KV, AT_Z, AT_MQ, AT_MZ = (n * W for n in range(5))

SUBLANES = 8
HALO = 16
TILE = 256
VMEM_LIMIT = 56 * 1024 * 1024

_NT = (((1,), (1,)), ((), ()))
_TN = (((0,), (0,)), ((), ()))


def _sigmoid(x):
    return 1.0 / (1.0 + jnp.exp(-x))


def _silu(x):
    return x * _sigmoid(x)


def _rms(x, g):
    return x * lax.rsqrt(jnp.mean(x * x, axis=-1, keepdims=True) + EPS) * g


def _params(sem):
    return pltpu.CompilerParams(dimension_semantics=sem, vmem_limit_bytes=VMEM_LIMIT)


def _col(ref, off, rows=slice(None)):
    return ref[0, rows, off:off + W]


def _memkv_kernel(m_ref, g_ref, w_ref, o_ref):
    h = _rms(m_ref[...], g_ref[0]).astype(BF16)
    o_ref[0] = jnp.dot(h, w_ref[0], preferred_element_type=F32).astype(BF16)


def _memkv(mem2, g, w, tm):
    depth = w.shape[0]
    t = mem2.shape[0]
    return pl.pallas_call(
        _memkv_kernel,
        grid=(depth, t // tm),
        in_specs=[
            pl.BlockSpec((tm, D_MODEL), lambda l, i: (i, 0)),
            pl.BlockSpec((1, 1, D_MODEL), lambda l, i: (l, 0, 0)),
            pl.BlockSpec((1, D_MODEL, 2 * W), lambda l, i: (l, 0, 0)),
        ],
        out_specs=pl.BlockSpec((1, tm, 2 * W), lambda l, i: (l, i, 0)),
        out_shape=jax.ShapeDtypeStruct((depth, t, 2 * W), BF16),
        compiler_params=_params(("parallel", "parallel")),
        name="memkv",
    )(mem2, g, w)


HG_CHUNK = 64
HG_LEVELS = (1, 2, 4, 8, 16, 32)


def _split3(x):
    x1 = x.astype(BF16)
    r1 = x - x1.astype(F32)
    x2 = r1.astype(BF16)
    x3 = (r1 - x2.astype(F32)).astype(BF16)
    return x1, x2, x3


def _dot3(m, x):
    x1, x2, x3 = _split3(x)
    acc = jnp.dot(m, x1, preferred_element_type=F32)
    acc = acc + jnp.dot(m, x2, preferred_element_type=F32)
    return acc + jnp.dot(m, x3, preferred_element_type=F32)


def _dot3_right(x, m):
    x1, x2, x3 = _split3(x)
    acc = jnp.dot(x1, m, preferred_element_type=F32)
    acc = acc + jnp.dot(x2, m, preferred_element_type=F32)
    return acc + jnp.dot(x3, m, preferred_element_type=F32)


def _hgrn_consts(reverse, ts):
    c_ = HG_CHUNK
    r = np.arange(c_)[:, None]
    c = np.arange(c_)[None, :]
    masks = []
    for m in HG_LEVELS:
        same = (r // (2 * m)) == (c // (2 * m))
        masks.append(same & (r % (2 * m) >= m) & (c % (2 * m) < m))
    masks.append(r == c)
    tri = c <= r
    if reverse:
        tri = tri[::-1, ::-1]
        masks = [m[::-1, ::-1] for m in masks]
    tri_tile = np.kron(np.eye(ts // c_), tri.astype(np.float32))
    pair = np.stack([np.tile(m, (1, N_HEADS)) for m in masks]).astype(np.float32)
    lane_head = np.arange(W) // HEAD_D
    head = np.stack([np.broadcast_to(lane_head == h, (c_, W)) for h in range(N_HEADS)])
    return (jnp.asarray(tri_tile, dtype=BF16), jnp.asarray(pair),
            jnp.asarray(head.astype(np.float32), dtype=BF16))


def _hgrn_const_specs(ts):
    const = lambda shape: pl.BlockSpec(shape, lambda bi, i: (0,) * len(shape))
    return [const((ts, ts)), const((len(HG_LEVELS) + 1, HG_CHUNK, W)),
            const((N_HEADS, HG_CHUNK, W))]


def _hgrn_scratch(ts):
    return [pltpu.VMEM((ts, W), F32), pltpu.VMEM((W, W), F32)]


def _is_query_row(m, reverse, row):
    second = (row // m) % 2 == 1
    return jnp.logical_not(second) if reverse else second


def _level_operand(q, kk, m, reverse, row):
    if m >= 8:
        parts = []
        for base in range(0, q.shape[0], m):
            second = (base // m) % 2 == 1
            parts.append((q if second != reverse else kk)[base:base + m])
        return jnp.concatenate(parts, axis=0)
    return jnp.where(_is_query_row(m, reverse, row), q, kk)


def _bcast_row(a, r, n):
    return jnp.broadcast_to(a[r:r + 1, :], (n, a.shape[1]))


def _level_decay(a, m, reverse):
    ref = m if reverse else m - 1
    if m >= 4:
        mid = jnp.concatenate([_bcast_row(a, base + ref, 2 * m)
                               for base in range(0, a.shape[0], 2 * m)], axis=0)
    else:
        low = lax.broadcasted_iota(jnp.int32, (8, a.shape[1]), 0) < 4
        mid = jnp.concatenate(
            [jnp.where(low, _bcast_row(a, base + ref, 8), _bcast_row(a, base + 4 + ref, 8))
             for base in range(0, a.shape[0], 8)], axis=0)
    return jnp.exp2(jnp.abs(a - mid) * (-LOG2E))


def _head_stack(xb, head_ref):
    return jnp.concatenate([xb * head_ref[h] for h in range(N_HEADS)], axis=0)


def _hgrn_scan(u_ref, f_off, lb_ref, tri_ref, pair_ref, head_ref, o_s, st_s, reverse):
    ts = u_ref.shape[1]
    nc = ts // HG_CHUNK
    nl = len(HG_LEVELS)
    chunk = lambda x, c: x[c * HG_CHUNK:(c + 1) * HG_CHUNK]
    tiled = lambda x: jnp.concatenate([x] * nc, axis=0)

    lb = lb_ref[...]
    f = lb + (1.0 - lb) * _sigmoid(_col(u_ref, f_off))
    kk = 1.0 - f
    q = _col(u_ref, HG_Q)
    v_b = _col(u_ref, HG_I).astype(BF16)
    a = _dot3(tri_ref[...], jnp.log(f))
    row = lax.broadcasted_iota(jnp.int32, (ts, W), 0)

    def scores(lhs_b, rhs_b):
        return jnp.concatenate(
            [lax.dot_general(chunk(lhs_b, c), _head_stack(chunk(rhs_b, c), head_ref), _NT,
                             preferred_element_type=F32) for c in range(nc)], axis=0)

    p = tiled(pair_ref[nl]) * scores(q.astype(BF16), kk.astype(BF16))
    for li, m in enumerate(HG_LEVELS):
        if m == 1:
            z = jnp.where(_is_query_row(1, reverse, row), q * f, kk)
        else:
            z = _level_decay(a, m, reverse) * _level_operand(q, kk, m, reverse, row)
        zb = z.astype(BF16)
        p = p + tiled(pair_ref[li]) * scores(zb, zb)
    pb = p.astype(BF16)
    o_in = [jnp.dot(chunk(pb, c), _head_stack(chunk(v_b, c), head_ref),
                    preferred_element_type=F32) for c in range(nc)]

    edge = 0 if reverse else HG_CHUNK - 1
    a_tot = jnp.concatenate([_bcast_row(a, c * HG_CHUNK + edge, HG_CHUNK) for c in range(nc)],
                            axis=0)
    qe = (q * jnp.exp(a)).astype(BF16)
    ke = (kk * jnp.exp(a_tot - a)).astype(BF16)
    dec = jnp.exp(a_tot)
    upd = [lax.dot_general(chunk(v_b, c), chunk(ke, c), _TN, preferred_element_type=F32)
           for c in range(nc)]
    same_head = (lax.broadcasted_iota(jnp.int32, (W, W), 0) // HEAD_D
                 == lax.broadcasted_iota(jnp.int32, (W, W), 1) // HEAD_D)
    st = st_s[...]
    for it in range(nc):
        c = (nc - 1 - it) if reverse else it
        st_m = jnp.where(same_head, st, 0.0).astype(BF16)
        o_s[pl.ds(c * HG_CHUNK, HG_CHUNK), :] = o_in[c] + lax.dot_general(
            chunk(qe, c), st_m, _NT, preferred_element_type=F32)
        st = st * dec[c * HG_CHUNK:c * HG_CHUNK + 1, :] + upd[c]
    st_s[...] = st


def _front_kernel(x_ref, g_ref, w_ref, lb_ref, tri_ref, pair_ref, head_ref,
                  uhg_ref, ucv_ref, uat_ref, ob_ref, o_s, st_s):
    @pl.when(pl.program_id(1) == 0)
    def _():
        st_s[...] = jnp.zeros_like(st_s)

    h = _rms(x_ref[0], g_ref[...]).astype(BF16)
    uhg_ref[0] = jnp.dot(h, w_ref[:, 0:G_HG], preferred_element_type=F32)
    ucv_ref[0] = jnp.dot(h, w_ref[:, G_HG:G_HG + G_CV], preferred_element_type=F32)
    uat_ref[0] = jnp.dot(h, w_ref[:, G_HG + G_CV:IN_WIDTH], preferred_element_type=F32)
    _hgrn_scan(uhg_ref, HG_FB, lb_ref, tri_ref, pair_ref, head_ref, o_s, st_s, reverse=True)
    ob_ref[0] = o_s[...]


def _front(x3, g, w, lb_b, ts):
    b, s, _ = x3.shape
    nt = s // ts
    const = lambda shape: pl.BlockSpec(shape, lambda bi, i: (0,) * len(shape))
    rev = lambda width: pl.BlockSpec((1, ts, width), lambda bi, i: (bi, nt - 1 - i, 0))
    return pl.pallas_call(
        _front_kernel,
        grid=(b, nt),
        in_specs=[rev(D_MODEL), const((1, D_MODEL)),
                  pl.BlockSpec((D_MODEL, IN_WIDTH), lambda bi, i: (0, 0),
                               pipeline_mode=pl.Buffered(1)),
                  const((1, W))] + _hgrn_const_specs(ts),
        out_specs=[rev(G_HG), rev(G_CV), rev(G_AT), rev(W)],
        out_shape=[jax.ShapeDtypeStruct((b, s, G_HG), F32),
                   jax.ShapeDtypeStruct((b, s, G_CV), F32),
                   jax.ShapeDtypeStruct((b, s, G_AT), F32),
                   jax.ShapeDtypeStruct((b, s, W), F32)],
        scratch_shapes=_hgrn_scratch(ts),
        compiler_params=_params(("parallel", "arbitrary")),
        name="front",
    )(x3, g, w, lb_b, *_hgrn_consts(True, ts))


CONV_ROWS = 64


def _depthwise(ext_ref, shift_ref, w_ref, ntap, first, ts, emit):
    if shift_ref is None:
        tap = lambda j, r0: ext_ref[pl.ds(first + r0 + j, CONV_ROWS), :]
    else:
        span = shift_ref.shape[1]
        for res in range(SUBLANES):
            shift_ref[res] = ext_ref[pl.ds(first + res, span), :]
        tap = lambda j, r0: shift_ref[j % SUBLANES,
                                      pl.ds(r0 + SUBLANES * (j // SUBLANES), CONV_ROWS), :]
    for r0 in range(0, ts, CONV_ROWS):
        acc = None
        for j in range(ntap):
            term = w_ref[j:j + 1, :] * tap(j, r0)
            acc = term if acc is None else acc + term
        emit(r0, acc)


def _conv_branches(cv, cv_p, cv_n, wdw, bdw, lng, lnb, scw, ycf, ysc,
                   ext_c, ext_s, shift_c, has_prev, has_next):
    ts = cv.shape[1]
    glu = lambda r: _col(r, CV_A) * _sigmoid(_col(r, CV_B))
    gate = lambda r: _col(r, CV_SC) * _col(r, CV_SV)
    ext_c[pl.ds(HALO, ts), :] = glu(cv)
    ext_c[pl.ds(0, HALO), :] = jnp.where(has_prev, glu(cv_p), 0.0)
    ext_c[pl.ds(HALO + ts, HALO), :] = jnp.where(has_next, glu(cv_n), 0.0)
    ext_s[pl.ds(HALO, ts), :] = gate(cv)
    ext_s[pl.ds(0, HALO), :] = jnp.where(has_prev, gate(cv_p), 0.0)
    ext_s[pl.ds(HALO + ts, HALO), :] = jnp.where(has_next, gate(cv_n), 0.0)

    def emit_conf(r0, acc):
        rows = pl.ds(r0, CONV_ROWS)
        c = acc + bdw[...]
        mu = jnp.mean(c, axis=-1, keepdims=True)
        cc = c - mu
        var = jnp.mean(cc * cc, axis=-1, keepdims=True)
        n = cc * lax.rsqrt(var + EPS) * lng[...] + lnb[...]
        ycf[rows, :] = (_silu(n) * _silu(_col(cv, CV_Z, rows))).astype(BF16)

    _depthwise(ext_c, shift_c, wdw, CONF_K, HALO - (CONF_K - 1) // 2, ts, emit_conf)

    def emit_short(r0, acc):
        rows = pl.ds(r0, CONV_ROWS)
        ysc[rows, :] = (_col(cv, CV_SB, rows) * acc
                        * _silu(_col(cv, CV_SZ, rows))).astype(BF16)

    _depthwise(ext_s, None, scw, SC_K, HALO - (SC_K - 1) // 2, ts, emit_short)


def _attend_t(units):
    lgs = [lax.dot_general(k, q, _NT, preferred_element_type=F32) for k, q, *_ in units]
    lgs = [lg if bias is None else lg + bias
           for lg, (_, _, _, bias, _, _) in zip(lgs, units)]
    lgs = [lg if drop is None else jnp.where(drop, -jnp.inf, lg)
           for lg, (_, _, _, _, _, drop) in zip(lgs, units)]
    ms = [jnp.max(lg, axis=0, keepdims=True) for lg in lgs]
    ms = [m if sink is None else jnp.maximum(m, sink)
          for m, (_, _, _, _, sink, _) in zip(ms, units)]
    es = [jnp.exp(lg - m) for lg, m in zip(lgs, ms)]
    dens = [jnp.sum(e, axis=0, keepdims=True) for e in es]
    dens = [den if sink is None else den + jnp.exp(sink - m)
            for den, m, (_, _, _, _, sink, _) in zip(dens, ms, units)]
    outs = [lax.dot_general(v, e.astype(BF16), _TN, preferred_element_type=F32)
            for e, (_, _, v, _, _, _) in zip(es, units)]
    return [o * (1.0 / den) for o, den in zip(outs, dens)]


def _attn_branches(at, kv_p, kv_n, mkv, bias_t, sink_rows, yat, ymem, kv_ext,
                   is_first, is_last):
    ts = at.shape[1]
    nq = ts // AT_BLOCK
    scale = HEAD_D ** -0.5
    group = AT_HEADS // AT_KV_HEADS

    kv_ext[pl.ds(0, AT_BLOCK), :] = kv_p[0].astype(BF16)
    kv_ext[pl.ds(AT_BLOCK, ts), :] = _col(at, AT_KV).astype(BF16)
    kv_ext[pl.ds(AT_BLOCK + ts, AT_BLOCK), :] = kv_n[0].astype(BF16)

    head = lambda x, h: x[:, h * HEAD_D:(h + 1) * HEAD_D]
    key_row = lax.broadcasted_iota(jnp.int32, (3 * AT_BLOCK, group * AT_BLOCK), 0)
    units = []
    for j in range(nq):
        band = kv_ext[pl.ds(j * AT_BLOCK, 3 * AT_BLOCK), :]
        drop = None
        if j == 0:
            drop = jnp.logical_and(is_first, key_row < AT_BLOCK)
        if j == nq - 1:
            hi = jnp.logical_and(is_last, key_row >= 2 * AT_BLOCK)
            drop = hi if drop is None else jnp.logical_or(drop, hi)
        q_blk = (_col(at, AT_Q, pl.ds(j * AT_BLOCK, AT_BLOCK)) * scale).astype(BF16)
        for kh in range(AT_KV_HEADS):
            q_rows = jnp.concatenate([head(q_blk, kh * group + g) for g in range(group)],
                                     axis=0)
            units.append((head(band, kh), q_rows, head(band, AT_KV_HEADS + kh),
                          bias_t[kh], sink_rows[kh], drop))
    q_m = (_col(at, AT_MQ) * scale).astype(BF16)
    kv_m = mkv[0]
    for h in range(AT_HEADS):
        units.append((head(kv_m, h), head(q_m, h), head(kv_m, AT_HEADS + h), None, None, None))

    outs = _attend_t(units)
    for j in range(nq):
        rows = pl.ds(j * AT_BLOCK, AT_BLOCK)
        parts = []
        for kh in range(AT_KV_HEADS):
            o_t = outs[j * AT_KV_HEADS + kh]
            parts += [o_t[:, g * AT_BLOCK:(g + 1) * AT_BLOCK] for g in range(group)]
        o = jnp.concatenate(parts, axis=0).T
        yat[rows, :] = (o * _silu(_col(at, AT_Z, rows))).astype(BF16)
    o = jnp.concatenate(outs[nq * AT_KV_HEADS:], axis=0).T
    ymem[...] = (o * _silu(_col(at, AT_MZ))).astype(BF16)


def _back_kernel(x_ref, gpre_ref,
                 hg_ref, ob_ref, lb_ref, tri_ref, pair_ref, head_ref, on_ref, ones_ref,
                 cv_ref, cvp_ref, cvn_ref, wdw, bdw, lng, lnb, scw,
                 at_ref, kvp_ref, kvn_ref, mkv_ref, bias_ref, sink_ref,
                 wg_ref, wb_ref, wo_ref, gpost_ref,
                 o_ref,
                 y_s, o_s, st_s, ext_c, ext_s, shift_c, kv_ext):
    i = pl.program_id(1)
    is_first = i == 0
    is_last = i == pl.num_programs(1) - 1

    @pl.when(is_first)
    def _():
        st_s[...] = jnp.zeros_like(st_s)

    _hgrn_scan(hg_ref, HG_FF, lb_ref, tri_ref, pair_ref, head_ref, o_s, st_s, reverse=False)
    o = o_s[...] + ob_ref[0]
    ms = _dot3_right(o * o, ones_ref[...]) * (1.0 / HEAD_D)
    on = o * lax.rsqrt(ms + EPS) * on_ref[...]
    y_s[0] = (on * _silu(_col(hg_ref, HG_Z))).astype(BF16)

    _conv_branches(cv_ref, cvp_ref, cvn_ref, wdw, bdw, lng, lnb, scw, y_s.at[1], y_s.at[2],
                   ext_c, ext_s, shift_c, jnp.logical_not(is_first), jnp.logical_not(is_last))
    _attn_branches(at_ref, kvp_ref, kvn_ref, mkv_ref, bias_ref, sink_ref,
                   y_s.at[3], y_s.at[4], kv_ext, is_first, is_last)

    x = x_ref[0]
    h = _rms(x, gpre_ref[...]).astype(BF16)
    merged = None
    for n in range(N_BRANCH):
        gate = _sigmoid(jnp.dot(h, wg_ref[n], preferred_element_type=F32))
        term = gate * jnp.dot(y_s[n], wb_ref[n], preferred_element_type=F32)
        merged = term if merged is None else merged + term
    z = jnp.dot(merged.astype(BF16), wo_ref[...], preferred_element_type=F32)
    o_ref[0] = x + _rms(z, gpost_ref[...])


def _back(x3, gpre, u_hg, o_bwd, lb_f, onorm_t, u_cv, wdw, bdw, lng, lnb, scw,
          u_at, mkv_l, bias, sink, wg, wb, wo, gpost, ts):
    b, s, _ = x3.shape
    mlen = mkv_l.shape[1]
    group = AT_HEADS // AT_KV_HEADS
    lane_head = np.arange(W) // HEAD_D
    ones_bd = jnp.asarray((lane_head[:, None] == lane_head[None, :]).astype(np.float32),
                          dtype=BF16)
    bias_t = jnp.transpose(bias.reshape(AT_KV_HEADS, group, AT_BLOCK, 3 * AT_BLOCK),
                           (0, 3, 1, 2)).reshape(AT_KV_HEADS, 3 * AT_BLOCK, group * AT_BLOCK)
    sink_rows = jnp.repeat(sink.astype(F32).reshape(AT_KV_HEADS, 1, group), AT_BLOCK, axis=2)

    const = lambda shape: pl.BlockSpec(shape, lambda bi, i: (0,) * len(shape))
    once = lambda shape: pl.BlockSpec(shape, lambda bi, i: (0,) * len(shape),
                                      pipeline_mode=pl.Buffered(1))
    cur = lambda width: pl.BlockSpec((1, ts, width), lambda bi, i: (bi, i, 0))

    def halo(rows, width, col, before):
        per, last = ts // rows, s // rows - 1
        if before:
            return pl.BlockSpec((1, rows, width),
                                lambda bi, i: (bi, jnp.maximum(i * per - 1, 0), col))
        return pl.BlockSpec((1, rows, width),
                            lambda bi, i: (bi, jnp.minimum((i + 1) * per, last), col))

    return pl.pallas_call(
        _back_kernel,
        grid=(b, s // ts),
        in_specs=[cur(D_MODEL), const((1, D_MODEL)),
                  cur(G_HG), cur(W), const((1, W))] + _hgrn_const_specs(ts) + [
                  const((1, W)), const((W, W)),
                  cur(G_CV), halo(HALO, G_CV, 0, True), halo(HALO, G_CV, 0, False),
                  const((CONF_K, W)), const((1, W)), const((1, W)), const((1, W)),
                  const((SC_K, W)),
                  cur(G_AT), halo(AT_BLOCK, W, AT_KV // W, True),
                  halo(AT_BLOCK, W, AT_KV // W, False),
                  pl.BlockSpec((1, mlen, 2 * W), lambda bi, i: (bi, 0, 0)),
                  const((AT_KV_HEADS, 3 * AT_BLOCK, group * AT_BLOCK)),
                  const((AT_KV_HEADS, 1, group * AT_BLOCK)),
                  once((N_BRANCH, D_MODEL, D_MODEL)), once((N_BRANCH, W, D_MODEL)),
                  once((D_MODEL, D_MODEL)), const((1, D_MODEL))],
        out_specs=cur(D_MODEL),
        out_shape=jax.ShapeDtypeStruct((b, s, D_MODEL), F32),
        scratch_shapes=[pltpu.VMEM((N_BRANCH, ts, W), BF16)] + _hgrn_scratch(ts) + [
            pltpu.VMEM((ts + 2 * HALO, W), F32), pltpu.VMEM((ts + 2 * HALO, W), F32),
            pltpu.VMEM((SUBLANES, ts + SUBLANES * ((CONF_K - 1) // SUBLANES), W), F32),
            pltpu.VMEM((ts + 2 * AT_BLOCK, W), BF16)],
        compiler_params=_params(("parallel", "arbitrary")),
        name="back",
    )(x3, gpre, u_hg, o_bwd, lb_f, *_hgrn_consts(False, ts), onorm_t, ones_bd,
      u_cv, u_cv, u_cv, wdw, bdw, lng, lnb, scw,
      u_at, u_at, u_at, mkv_l, bias_t, sink_rows, wg, wb, wo, gpost)


def _t5_bucket(rel):
    half = REL_BUCKETS // 2
    n = -rel
    ret = jnp.where(n < 0, half, 0)
    n = jnp.abs(n)
    max_exact = half // 2
    large = max_exact + (jnp.log(jnp.maximum(n, 1).astype(F32) / max_exact)
                         / math.log(REL_MAX_DIST / max_exact)
                         * (half - max_exact)).astype(jnp.int32)
    large = jnp.minimum(large, half - 1)
    return ret + jnp.where(n < max_exact, n, large)


def _banded_bias(rel_bias):
    qi = jnp.arange(AT_BLOCK)[:, None]
    sj = jnp.arange(3 * AT_BLOCK)[None, :]
    rel = (sj - AT_BLOCK) - qi
    onehot = (_t5_bucket(rel)[..., None] == jnp.arange(REL_BUCKETS)).astype(F32)
    bias = jnp.einsum("qsb,bh->hqs", onehot, rel_bias.astype(F32),
                      precision=lax.Precision.HIGHEST)
    return jnp.where((jnp.abs(rel) <= AT_WINDOW)[None], bias, -jnp.inf)


def kernel(x, mem, norm_pre, norm_post, w_in, hg_lb_logits, hg_onorm, conf_dw_w, conf_dw_b,
           conf_ln_g, conf_ln_b, sc_w, attn_sink, rel_bias, mem_norm, w_mem_kv, w_gate,
           w_branch, w_out):
    b, s, d = x.shape
    depth = w_in.shape[0]
    assert d == D_MODEL and w_in.shape[2] == IN_WIDTH
    ts = min(TILE, s)

    lb_soft = jax.nn.softmax(hg_lb_logits.astype(F32), axis=0)
    lower = jnp.cumsum(lb_soft, axis=0) - lb_soft[0]
    bias = _banded_bias(rel_bias)
    onorm_t = jnp.tile(hg_onorm.astype(F32), (1, N_HEADS))

    w_in_b = w_in.astype(BF16)
    w_gate_b = w_gate.astype(BF16)
    w_branch_b = w_branch.astype(BF16)
    w_out_b = w_out.astype(BF16)

    m = mem.shape[1]
    mkv = _memkv(mem.reshape(b * m, d), mem_norm.reshape(depth, 1, d),
                 w_mem_kv.astype(BF16), min(256, b * m)).reshape(depth, b, m, 2 * W)

    row = lambda v: v.reshape(1, -1)
    for l in range(depth):
        u_hg, u_cv, u_at, o_bwd = _front(x, row(norm_pre[l]), w_in_b[l], row(lower[l, 1]), ts)
        x = _back(x, row(norm_pre[l]), u_hg, o_bwd, row(lower[l, 0]), row(onorm_t[l]),
                  u_cv, conf_dw_w[l], row(conf_dw_b[l]), row(conf_ln_g[l]), row(conf_ln_b[l]),
                  sc_w[l], u_at, mkv[l], bias, attn_sink[l],
                  w_gate_b[l], w_branch_b[l], w_out_b[l], row(norm_post[l]), ts)
    return x
```

```python
import math

import jax
import jax.numpy as jnp
import numpy as np
from jax import lax
from jax.experimental import pallas as pl
from jax.experimental.pallas import tpu as pltpu

F32 = jnp.float32
BF16 = jnp.bfloat16

D_MODEL = 1024
W = 256
N_BRANCH = 5
HEAD_D = 64
N_HEADS = W // HEAD_D
CONF_K = 31
SC_K = 3
AT_HEADS = 4
AT_KV_HEADS = 2
AT_BLOCK = 128
AT_WINDOW = 128
REL_BUCKETS = 32
REL_MAX_DIST = 128
EPS = 1e-6
LOG2E = math.log2(math.e)

G_HG, G_CV, G_AT = 5 * W, 7 * W, 5 * W
IN_WIDTH = G_HG + G_CV + G_AT
HG_Q, HG_FF, HG_FB, HG_I, HG_Z = (n * W for n in range(5))
CV_A, CV_B, CV_Z, CV_SB, CV_SC, CV_SV, CV_SZ = (n * W for n in range(7))
AT_Q, AT_KV, AT_Z, AT_MQ, AT_MZ = (n * W for n in range(5))

SUBLANES = 8
HALO = 16
TILE = 256
MEMKV_ROWS = 1024
VMEM_LIMIT = 56 * 1024 * 1024

_NT = (((1,), (1,)), ((), ()))
_TN = (((0,), (0,)), ((), ()))


def _sigmoid(x):
    return 1.0 / (1.0 + jnp.exp(-x))


def _silu(x):
    return x * _sigmoid(x)


def _rms(x, g):
    return x * lax.rsqrt(jnp.mean(x * x, axis=-1, keepdims=True) + EPS) * g


def _params(sem):
    return pltpu.CompilerParams(dimension_semantics=sem, vmem_limit_bytes=VMEM_LIMIT)


def _col(ref, off, rows=slice(None)):
    return ref[0, rows, off:off + W]


def _memkv_kernel(m_ref, g_ref, w_ref, o_ref):
    h = _rms(m_ref[...], g_ref[0]).astype(BF16)
    o_ref[0] = jnp.dot(h, w_ref[0], preferred_element_type=F32).astype(BF16)


def _memkv(mem2, g, w, tm):
    depth = w.shape[0]
    t = mem2.shape[0]
    return pl.pallas_call(
        _memkv_kernel,
        grid=(depth, t // tm),
        in_specs=[
            pl.BlockSpec((tm, D_MODEL), lambda l, i: (i, 0)),
            pl.BlockSpec((1, 1, D_MODEL), lambda l, i: (l, 0, 0)),
            pl.BlockSpec((1, D_MODEL, 2 * W), lambda l, i: (l, 0, 0)),
        ],
        out_specs=pl.BlockSpec((1, tm, 2 * W), lambda l, i: (l, i, 0)),
        out_shape=jax.ShapeDtypeStruct((depth, t, 2 * W), BF16),
        compiler_params=_params(("parallel", "parallel")),
        name="memkv",
    )(mem2, g, w)


HG_CHUNK = 64
HG_LEVELS = (1, 2, 4, 8, 16, 32)


def _split3(x):
    x1 = x.astype(BF16)
    r1 = x - x1.astype(F32)
    x2 = r1.astype(BF16)
    x3 = (r1 - x2.astype(F32)).astype(BF16)
    return x1, x2, x3


def _dot3(m, x):
    x1, x2, x3 = _split3(x)
    acc = jnp.dot(m, x1, preferred_element_type=F32)
    acc = acc + jnp.dot(m, x2, preferred_element_type=F32)
    return acc + jnp.dot(m, x3, preferred_element_type=F32)


def _dot3_right(x, m):
    x1, x2, x3 = _split3(x)
    acc = jnp.dot(x1, m, preferred_element_type=F32)
    acc = acc + jnp.dot(x2, m, preferred_element_type=F32)
    return acc + jnp.dot(x3, m, preferred_element_type=F32)


def _hgrn_consts(reverse, ts):
    c_ = HG_CHUNK
    r = np.arange(c_)[:, None]
    c = np.arange(c_)[None, :]
    masks = []
    for m in HG_LEVELS:
        same = (r // (2 * m)) == (c // (2 * m))
        masks.append(same & (r % (2 * m) >= m) & (c % (2 * m) < m))
    masks.append(r == c)
    tri = c <= r
    if reverse:
        tri = tri[::-1, ::-1]
        masks = [m[::-1, ::-1] for m in masks]
    tri_tile = np.kron(np.eye(ts // c_), tri.astype(np.float32))
    pair = np.stack([np.tile(m, (1, N_HEADS)) for m in masks]).astype(np.float32)
    lane_head = np.arange(W) // HEAD_D
    head = np.stack([np.broadcast_to(lane_head == h, (c_, W)) for h in range(N_HEADS)])
    return (jnp.asarray(tri_tile, dtype=BF16), jnp.asarray(pair),
            jnp.asarray(head.astype(np.float32), dtype=BF16))


def _hgrn_const_specs(ts):
    const = lambda shape: pl.BlockSpec(shape, lambda bi, i: (0,) * len(shape))
    return [const((ts, ts)), const((len(HG_LEVELS) + 1, HG_CHUNK, W)),
            const((N_HEADS, HG_CHUNK, W))]


def _hgrn_scratch(ts):
    return [pltpu.VMEM((ts, W), F32), pltpu.VMEM((W, W), F32)]


def _is_query_row(m, reverse, row):
    second = (row // m) % 2 == 1
    return jnp.logical_not(second) if reverse else second


def _level_operand(q, kk, m, reverse, row):
    if m >= 8:
        parts = []
        for base in range(0, q.shape[0], m):
            second = (base // m) % 2 == 1
            parts.append((q if second != reverse else kk)[base:base + m])
        return jnp.concatenate(parts, axis=0)
    return jnp.where(_is_query_row(m, reverse, row), q, kk)


def _bcast_row(a, r, n):
    return jnp.broadcast_to(a[r:r + 1, :], (n, a.shape[1]))


def _level_decay(a, m, reverse):
    ref = m if reverse else m - 1
    if m >= 4:
        mid = jnp.concatenate([_bcast_row(a, base + ref, 2 * m)
                               for base in range(0, a.shape[0], 2 * m)], axis=0)
    else:
        low = lax.broadcasted_iota(jnp.int32, (8, a.shape[1]), 0) < 4
        mid = jnp.concatenate(
            [jnp.where(low, _bcast_row(a, base + ref, 8), _bcast_row(a, base + 4 + ref, 8))
             for base in range(0, a.shape[0], 8)], axis=0)
    return jnp.exp2(jnp.abs(a - mid) * (-LOG2E))


def _head_stack(xb, head_ref):
    return jnp.concatenate([xb * head_ref[h] for h in range(N_HEADS)], axis=0)


def _hgrn_scan(u_ref, f_off, lb_ref, tri_ref, pair_ref, head_ref, o_s, st_s, reverse):
    ts = u_ref.shape[1]
    nc = ts // HG_CHUNK
    nl = len(HG_LEVELS)
    chunk = lambda x, c: x[c * HG_CHUNK:(c + 1) * HG_CHUNK]
    tiled = lambda x: jnp.concatenate([x] * nc, axis=0)

    lb = lb_ref[...]
    f = lb + (1.0 - lb) * _sigmoid(_col(u_ref, f_off))
    kk = 1.0 - f
    q = _col(u_ref, HG_Q)
    v_b = _col(u_ref, HG_I).astype(BF16)
    a = _dot3(tri_ref[...], jnp.log(f))
    row = lax.broadcasted_iota(jnp.int32, (ts, W), 0)

    def scores(lhs_b, rhs_b):
        return jnp.concatenate(
            [lax.dot_general(chunk(lhs_b, c), _head_stack(chunk(rhs_b, c), head_ref), _NT,
                             preferred_element_type=F32) for c in range(nc)], axis=0)

    p = tiled(pair_ref[nl]) * scores(q.astype(BF16), kk.astype(BF16))
    for li, m in enumerate(HG_LEVELS):
        if m == 1:
            z = jnp.where(_is_query_row(1, reverse, row), q * f, kk)
        else:
            z = _level_decay(a, m, reverse) * _level_operand(q, kk, m, reverse, row)
        zb = z.astype(BF16)
        p = p + tiled(pair_ref[li]) * scores(zb, zb)
    pb = p.astype(BF16)
    o_in = [jnp.dot(chunk(pb, c), _head_stack(chunk(v_b, c), head_ref),
                    preferred_element_type=F32) for c in range(nc)]

    edge = 0 if reverse else HG_CHUNK - 1
    a_tot = jnp.concatenate([_bcast_row(a, c * HG_CHUNK + edge, HG_CHUNK) for c in range(nc)],
                            axis=0)
    qe = (q * jnp.exp(a)).astype(BF16)
    ke = (kk * jnp.exp(a_tot - a)).astype(BF16)
    dec = jnp.exp(a_tot)
    upd = [lax.dot_general(chunk(v_b, c), chunk(ke, c), _TN, preferred_element_type=F32)
           for c in range(nc)]
    same_head = (lax.broadcasted_iota(jnp.int32, (W, W), 0) // HEAD_D
                 == lax.broadcasted_iota(jnp.int32, (W, W), 1) // HEAD_D)
    st = st_s[...]
    for it in range(nc):
        c = (nc - 1 - it) if reverse else it
        st_m = jnp.where(same_head, st, 0.0).astype(BF16)
        o_s[pl.ds(c * HG_CHUNK, HG_CHUNK), :] = o_in[c] + lax.dot_general(
            chunk(qe, c), st_m, _NT, preferred_element_type=F32)
        st = st * dec[c * HG_CHUNK:c * HG_CHUNK + 1, :] + upd[c]
    st_s[...] = st


def _front_kernel(x_ref, g_ref, w_ref, lb_ref, tri_ref, pair_ref, head_ref,
                  uhg_ref, ucv_ref, uat_ref, ob_ref, o_s, st_s):
    @pl.when(pl.program_id(1) == 0)
    def _():
        st_s[...] = jnp.zeros_like(st_s)

    h = _rms(x_ref[0], g_ref[...]).astype(BF16)
    uhg_ref[0] = jnp.dot(h, w_ref[0, :, 0:G_HG], preferred_element_type=F32)
    ucv_ref[0] = jnp.dot(h, w_ref[0, :, G_HG:G_HG + G_CV], preferred_element_type=F32)
    uat_ref[0] = jnp.dot(h, w_ref[0, :, G_HG + G_CV:IN_WIDTH], preferred_element_type=F32)
    _hgrn_scan(uhg_ref, HG_FB, lb_ref, tri_ref, pair_ref, head_ref, o_s, st_s, reverse=True)
    ob_ref[0] = o_s[...]


def _front(x3, g, w_all, layer, lb_b, ts):
    b, s, _ = x3.shape
    nt = s // ts
    const = lambda shape: pl.BlockSpec(shape, lambda bi, i: (0,) * len(shape))
    rev = lambda width: pl.BlockSpec((1, ts, width), lambda bi, i: (bi, nt - 1 - i, 0))
    return pl.pallas_call(
        _front_kernel,
        grid=(b, nt),
        in_specs=[rev(D_MODEL), const((1, D_MODEL)),
                  pl.BlockSpec((1, D_MODEL, IN_WIDTH), lambda bi, i: (layer, 0, 0),
                               pipeline_mode=pl.Buffered(1)),
                  const((1, W))] + _hgrn_const_specs(ts),
        out_specs=[rev(G_HG), rev(G_CV), rev(G_AT), rev(W)],
        out_shape=[jax.ShapeDtypeStruct((b, s, G_HG), F32),
                   jax.ShapeDtypeStruct((b, s, G_CV), F32),
                   jax.ShapeDtypeStruct((b, s, G_AT), F32),
                   jax.ShapeDtypeStruct((b, s, W), F32)],
        scratch_shapes=_hgrn_scratch(ts),
        compiler_params=_params(("parallel", "arbitrary")),
        name="front",
    )(x3, g, w_all, lb_b, *_hgrn_consts(True, ts))


CONV_ROWS = 64


def _depthwise(ext_ref, shift_ref, w_ref, ntap, first, ts, emit):
    if shift_ref is None:
        tap = lambda j, r0: ext_ref[pl.ds(first + r0 + j, CONV_ROWS), :]
    else:
        span = shift_ref.shape[1]
        for res in range(SUBLANES):
            shift_ref[res] = ext_ref[pl.ds(first + res, span), :]
        tap = lambda j, r0: shift_ref[j % SUBLANES,
                                      pl.ds(r0 + SUBLANES * (j // SUBLANES), CONV_ROWS), :]
    for r0 in range(0, ts, CONV_ROWS):
        acc = None
        for j in range(ntap):
            term = w_ref[j:j + 1, :] * tap(j, r0)
            acc = term if acc is None else acc + term
        emit(r0, acc)


def _conv_branches(cv, cv_p, cv_n, wdw, bdw, lng, lnb, scw, ycf, ysc,
                   ext_c, ext_s, shift_c, has_prev, has_next):
    ts = cv.shape[1]
    glu = lambda r: _col(r, CV_A) * _sigmoid(_col(r, CV_B))
    gate = lambda r: _col(r, CV_SC) * _col(r, CV_SV)
    ext_c[pl.ds(HALO, ts), :] = glu(cv)
    ext_c[pl.ds(0, HALO), :] = jnp.where(has_prev, glu(cv_p), 0.0)
    ext_c[pl.ds(HALO + ts, HALO), :] = jnp.where(has_next, glu(cv_n), 0.0)
    ext_s[pl.ds(HALO, ts), :] = gate(cv)
    ext_s[pl.ds(0, HALO), :] = jnp.where(has_prev, gate(cv_p), 0.0)
    ext_s[pl.ds(HALO + ts, HALO), :] = jnp.where(has_next, gate(cv_n), 0.0)

    def emit_conf(r0, acc):
        rows = pl.ds(r0, CONV_ROWS)
        c = acc + bdw[...]
        mu = jnp.mean(c, axis=-1, keepdims=True)
        cc = c - mu
        var = jnp.mean(cc * cc, axis=-1, keepdims=True)
        n = cc * lax.rsqrt(var + EPS) * lng[...] + lnb[...]
        ycf[rows, :] = (_silu(n) * _silu(_col(cv, CV_Z, rows))).astype(BF16)

    _depthwise(ext_c, shift_c, wdw, CONF_K, HALO - (CONF_K - 1) // 2, ts, emit_conf)

    def emit_short(r0, acc):
        rows = pl.ds(r0, CONV_ROWS)
        ysc[rows, :] = (_col(cv, CV_SB, rows) * acc
                        * _silu(_col(cv, CV_SZ, rows))).astype(BF16)

    _depthwise(ext_s, None, scw, SC_K, HALO - (SC_K - 1) // 2, ts, emit_short)


def _attend_t(units):
    lgs = [lax.dot_general(k, q, _NT, preferred_element_type=F32) for k, q, *_ in units]
    lgs = [lg if bias is None else lg + bias
           for lg, (_, _, _, bias, _, _) in zip(lgs, units)]
    lgs = [lg if drop is None else jnp.where(drop, -jnp.inf, lg)
           for lg, (_, _, _, _, _, drop) in zip(lgs, units)]
    ms = [jnp.max(lg, axis=0, keepdims=True) for lg in lgs]
    ms = [m if sink is None else jnp.maximum(m, sink)
          for m, (_, _, _, _, sink, _) in zip(ms, units)]
    es = [jnp.exp(lg - m) for lg, m in zip(lgs, ms)]
    dens = [jnp.sum(e, axis=0, keepdims=True) for e in es]
    dens = [den if sink is None else den + jnp.exp(sink - m)
            for den, m, (_, _, _, _, sink, _) in zip(dens, ms, units)]
    outs = [lax.dot_general(v, e.astype(BF16), _TN, preferred_element_type=F32)
            for e, (_, _, v, _, _, _) in zip(es, units)]
    return [o * (1.0 / den) for o, den in zip(outs, dens)]


def _attn_branches(at, kv_p, kv_n, mkv, bias_t, sink_rows, yat, ymem, kv_ext,
                   is_first, is_last):
    ts = at.shape[1]
    nq = ts // AT_BLOCK
    scale = HEAD_D ** -0.5
    group = AT_HEADS // AT_KV_HEADS

    kv_ext[pl.ds(0, AT_BLOCK), :] = kv_p[0].astype(BF16)
    kv_ext[pl.ds(AT_BLOCK, ts), :] = _col(at, AT_KV).astype(BF16)
    kv_ext[pl.ds(AT_BLOCK + ts, AT_BLOCK), :] = kv_n[0].astype(BF16)

    head = lambda x, h: x[:, h * HEAD_D:(h + 1) * HEAD_D]
    key_row = lax.broadcasted_iota(jnp.int32, (3 * AT_BLOCK, group * AT_BLOCK), 0)
    units = []
    for j in range(nq):
        band = kv_ext[pl.ds(j * AT_BLOCK, 3 * AT_BLOCK), :]
        drop = None
        if j == 0:
            drop = jnp.logical_and(is_first, key_row < AT_BLOCK)
        if j == nq - 1:
            hi = jnp.logical_and(is_last, key_row >= 2 * AT_BLOCK)
            drop = hi if drop is None else jnp.logical_or(drop, hi)
        q_blk = (_col(at, AT_Q, pl.ds(j * AT_BLOCK, AT_BLOCK)) * scale).astype(BF16)
        for kh in range(AT_KV_HEADS):
            q_rows = jnp.concatenate([head(q_blk, kh * group + g) for g in range(group)],
                                     axis=0)
            units.append((head(band, kh), q_rows, head(band, AT_KV_HEADS + kh),
                          bias_t[kh], sink_rows[kh], drop))
    q_m = (_col(at, AT_MQ) * scale).astype(BF16)
    kv_m = mkv[0, 0]
    for h in range(AT_HEADS):
        units.append((head(kv_m, h), head(q_m, h), head(kv_m, AT_HEADS + h), None, None, None))

    outs = _attend_t(units)
    for j in range(nq):
        rows = pl.ds(j * AT_BLOCK, AT_BLOCK)
        parts = []
        for kh in range(AT_KV_HEADS):
            o_t = outs[j * AT_KV_HEADS + kh]
            parts += [o_t[:, g * AT_BLOCK:(g + 1) * AT_BLOCK] for g in range(group)]
        o = jnp.concatenate(parts, axis=0).T
        yat[rows, :] = (o * _silu(_col(at, AT_Z, rows))).astype(BF16)
    o = jnp.concatenate(outs[nq * AT_KV_HEADS:], axis=0).T
    ymem[...] = (o * _silu(_col(at, AT_MZ))).astype(BF16)


def _back_kernel(x_ref, gpre_ref,
                 hg_ref, ob_ref, lb_ref, tri_ref, pair_ref, head_ref, on_ref, ones_ref,
                 cv_ref, cvp_ref, cvn_ref, wdw, bdw, lng, lnb, scw,
                 at_ref, kvp_ref, kvn_ref, mkv_ref, bias_ref, sink_ref,
                 wg_ref, wb_ref, wo_ref, gpost_ref,
                 o_ref,
                 y_s, o_s, st_s, ext_c, ext_s, shift_c, kv_ext):
    i = pl.program_id(1)
    is_first = i == 0
    is_last = i == pl.num_programs(1) - 1

    @pl.when(is_first)
    def _():
        st_s[...] = jnp.zeros_like(st_s)

    _hgrn_scan(hg_ref, HG_FF, lb_ref, tri_ref, pair_ref, head_ref, o_s, st_s, reverse=False)
    o = o_s[...] + ob_ref[0]
    ms = _dot3_right(o * o, ones_ref[...]) * (1.0 / HEAD_D)
    on = o * lax.rsqrt(ms + EPS) * on_ref[...]
    y_s[0] = (on * _silu(_col(hg_ref, HG_Z))).astype(BF16)

    _conv_branches(cv_ref, cvp_ref, cvn_ref, wdw, bdw, lng, lnb, scw, y_s.at[1], y_s.at[2],
                   ext_c, ext_s, shift_c, jnp.logical_not(is_first), jnp.logical_not(is_last))
    _attn_branches(at_ref, kvp_ref, kvn_ref, mkv_ref, bias_ref, sink_ref,
                   y_s.at[3], y_s.at[4], kv_ext, is_first, is_last)

    x = x_ref[0]
    h = _rms(x, gpre_ref[...]).astype(BF16)
    merged = None
    for n in range(N_BRANCH):
        gate = _sigmoid(jnp.dot(h, wg_ref[0, n], preferred_element_type=F32))
        term = gate * jnp.dot(y_s[n], wb_ref[0, n], preferred_element_type=F32)
        merged = term if merged is None else merged + term
    z = jnp.dot(merged.astype(BF16), wo_ref[0], preferred_element_type=F32)
    o_ref[0] = x + _rms(z, gpost_ref[...])


def _back(x3, gpre, u_hg, o_bwd, lb_f, onorm_t, u_cv, wdw, bdw, lng, lnb, scw,
          u_at, mkv, bias, sink, wg_all, wb_all, wo_all, layer, gpost, ts):
    b, s, _ = x3.shape
    mlen = mkv.shape[2]
    group = AT_HEADS // AT_KV_HEADS
    lane_head = np.arange(W) // HEAD_D
    ones_bd = jnp.asarray((lane_head[:, None] == lane_head[None, :]).astype(np.float32),
                          dtype=BF16)
    bias_t = jnp.transpose(bias.reshape(AT_KV_HEADS, group, AT_BLOCK, 3 * AT_BLOCK),
                           (0, 3, 1, 2)).reshape(AT_KV_HEADS, 3 * AT_BLOCK, group * AT_BLOCK)
    sink_rows = jnp.repeat(sink.astype(F32).reshape(AT_KV_HEADS, 1, group), AT_BLOCK, axis=2)

    const = lambda shape: pl.BlockSpec(shape, lambda bi, i: (0,) * len(shape))
    once = lambda shape: pl.BlockSpec((1,) + shape, lambda bi, i: (layer,) + (0,) * len(shape),
                                      pipeline_mode=pl.Buffered(1))
    cur = lambda width: pl.BlockSpec((1, ts, width), lambda bi, i: (bi, i, 0))

    def halo(rows, width, col, before):
        per, last = ts // rows, s // rows - 1
        if before:
            return pl.BlockSpec((1, rows, width),
                                lambda bi, i: (bi, jnp.maximum(i * per - 1, 0), col))
        return pl.BlockSpec((1, rows, width),
                            lambda bi, i: (bi, jnp.minimum((i + 1) * per, last), col))

    return pl.pallas_call(
        _back_kernel,
        grid=(b, s // ts),
        in_specs=[cur(D_MODEL), const((1, D_MODEL)),
                  cur(G_HG), cur(W), const((1, W))] + _hgrn_const_specs(ts) + [
                  const((1, W)), const((W, W)),
                  cur(G_CV), halo(HALO, G_CV, 0, True), halo(HALO, G_CV, 0, False),
                  const((CONF_K, W)), const((1, W)), const((1, W)), const((1, W)),
                  const((SC_K, W)),
                  cur(G_AT), halo(AT_BLOCK, W, AT_KV // W, True),
                  halo(AT_BLOCK, W, AT_KV // W, False),
                  pl.BlockSpec((1, 1, mlen, 2 * W), lambda bi, i: (layer, bi, 0, 0)),
                  const((AT_KV_HEADS, 3 * AT_BLOCK, group * AT_BLOCK)),
                  const((AT_KV_HEADS, 1, group * AT_BLOCK)),
                  once((N_BRANCH, D_MODEL, D_MODEL)), once((N_BRANCH, W, D_MODEL)),
                  once((D_MODEL, D_MODEL)), const((1, D_MODEL))],
        out_specs=cur(D_MODEL),
        out_shape=jax.ShapeDtypeStruct((b, s, D_MODEL), F32),
        scratch_shapes=[pltpu.VMEM((N_BRANCH, ts, W), BF16)] + _hgrn_scratch(ts) + [
            pltpu.VMEM((ts + 2 * HALO, W), F32), pltpu.VMEM((ts + 2 * HALO, W), F32),
            pltpu.VMEM((SUBLANES, ts + SUBLANES * ((CONF_K - 1) // SUBLANES), W), F32),
            pltpu.VMEM((ts + 2 * AT_BLOCK, W), BF16)],
        compiler_params=_params(("parallel", "arbitrary")),
        name="back",
    )(x3, gpre, u_hg, o_bwd, lb_f, *_hgrn_consts(False, ts), onorm_t, ones_bd,
      u_cv, u_cv, u_cv, wdw, bdw, lng, lnb, scw,
      u_at, u_at, u_at, mkv, bias_t, sink_rows, wg_all, wb_all, wo_all, gpost)


def _t5_bucket(rel):
    half = REL_BUCKETS // 2
    n = -rel
    ret = jnp.where(n < 0, half, 0)
    n = jnp.abs(n)
    max_exact = half // 2
    large = max_exact + (jnp.log(jnp.maximum(n, 1).astype(F32) / max_exact)
                         / math.log(REL_MAX_DIST / max_exact)
                         * (half - max_exact)).astype(jnp.int32)
    large = jnp.minimum(large, half - 1)
    return ret + jnp.where(n < max_exact, n, large)


def _banded_bias(rel_bias):
    qi = jnp.arange(AT_BLOCK)[:, None]
    sj = jnp.arange(3 * AT_BLOCK)[None, :]
    rel = (sj - AT_BLOCK) - qi
    onehot = (_t5_bucket(rel)[..., None] == jnp.arange(REL_BUCKETS)).astype(F32)
    bias = jnp.einsum("qsb,bh->hqs", onehot, rel_bias.astype(F32),
                      precision=lax.Precision.HIGHEST)
    return jnp.where((jnp.abs(rel) <= AT_WINDOW)[None], bias, -jnp.inf)


def kernel(x, mem, norm_pre, norm_post, w_in, hg_lb_logits, hg_onorm, conf_dw_w, conf_dw_b,
           conf_ln_g, conf_ln_b, sc_w, attn_sink, rel_bias, mem_norm, w_mem_kv, w_gate,
           w_branch, w_out):
    b, s, d = x.shape
    depth = w_in.shape[0]
    assert d == D_MODEL and w_in.shape[2] == IN_WIDTH
    ts = min(TILE, s)

    lb_soft = jax.nn.softmax(hg_lb_logits.astype(F32), axis=0)
    lower = jnp.cumsum(lb_soft, axis=0) - lb_soft[0]
    bias = _banded_bias(rel_bias)
    onorm_t = jnp.tile(hg_onorm.astype(F32), (1, N_HEADS))

    w_in_b = w_in.astype(BF16)
    w_gate_b = w_gate.astype(BF16)
    w_branch_b = w_branch.astype(BF16)
    w_out_b = w_out.astype(BF16)

    m = mem.shape[1]
    mkv = _memkv(mem.reshape(b * m, d), mem_norm.reshape(depth, 1, d),
                 w_mem_kv.astype(BF16), min(MEMKV_ROWS, b * m)).reshape(depth, b, m, 2 * W)

    row = lambda v: v.reshape(1, -1)
    for l in range(depth):
        u_hg, u_cv, u_at, o_bwd = _front(x, row(norm_pre[l]), w_in_b, l, row(lower[l, 1]), ts)
        x = _back(x, row(norm_pre[l]), u_hg, o_bwd, row(lower[l, 0]), row(onorm_t[l]),
                  u_cv, conf_dw_w[l], row(conf_dw_b[l]), row(conf_ln_g[l]), row(conf_ln_b[l]),
                  sc_w[l], u_at, mkv, bias, attn_sink[l],
                  w_gate_b, w_branch_b, w_out_b, l, row(norm_post[l]), ts)
    return x
```
